```python
import jax
import jax.numpy as jnp
from jax import lax
import numpy as np

D_MODEL = 1024
BATCH = 4
SEQ = 4096
DEPTH = 4
DEC_BATCH = 128
DEC_SEQ = 8
PAST_LEN = 2048
PAGE_SIZE = 128

N_MIXERS = 3
N_MOBA_LAYERS = (DEPTH + 2) // N_MIXERS
N_HGRN_LAYERS = (DEPTH + 1) // N_MIXERS
N_DIL_LAYERS = DEPTH // N_MIXERS

MOBA_HD = 64
MOBA_HEADS = D_MODEL // MOBA_HD
MOBA_BLOCK = 256
MOBA_TOPK = 3
MOBA_ROW_CHUNK = 64

HGRN_EXPAND = 128
HGRN_HEADS = D_MODEL // HGRN_EXPAND
HGRN_KD = HGRN_EXPAND
HGRN_VD = D_MODEL // HGRN_HEADS
HGRN_CHUNK = 32

DIL_PAIRS = ((128, 1), (512, 4), (2048, 16))
N_DIL_GROUPS = len(DIL_PAIRS)
DIL_HEADS = 8
DIL_HD = D_MODEL // DIL_HEADS
DIL_WIDTH = DIL_HEADS * DIL_HD

D_FF = 4 * D_MODEL
ROPE_THETA = 10000.0
NORM_EPS = 1e-6

kernel_name = 'hybrid_moba_hgrn2_dilated_decoder_step'


def rms_norm(x, g):
    xf = x.astype(jnp.float32)
    y = xf * lax.rsqrt(jnp.mean(xf * xf, axis=-1, keepdims=True) + NORM_EPS)
    return (y * g.astype(jnp.float32)).astype(x.dtype)


def rope(x, pos):
    hd = x.shape[-1]
    inv = ROPE_THETA ** (-jnp.arange(0, hd, 2, dtype=jnp.float32) / hd)
    ang = pos.astype(jnp.float32)[:, None] * inv[None, :]
    cos = jnp.cos(ang)[None, :, None, :]
    sin = jnp.sin(ang)[None, :, None, :]
    x1, x2 = jnp.split(x.astype(jnp.float32), 2, axis=-1)
    return jnp.concatenate([x1 * cos - x2 * sin, x2 * cos + x1 * sin], axis=-1).astype(x.dtype)


def modulation(c, w_ada, b_ada):
    mod = jax.nn.silu(c) @ w_ada + b_ada
    return jnp.split(mod[:, None, :], 6, axis=-1)


def sq_relu_mlp(h, w1, w2):
    return jnp.square(jax.nn.relu(h @ w1)) @ w2


def moba_attend(q, k_full, v_full, q_pos):
    B, T, H, hd = q.shape
    L = k_full.shape[1]
    n_blk = -(-L // MOBA_BLOCK)
    pad = n_blk * MOBA_BLOCK - L
    kb = jnp.pad(k_full, ((0, 0), (0, pad), (0, 0), (0, 0))).reshape(B, n_blk, MOBA_BLOCK, H, hd)
    vb = jnp.pad(v_full, ((0, 0), (0, pad), (0, 0), (0, 0))).reshape(B, n_blk, MOBA_BLOCK, H, hd)
    k_mean = jnp.mean(kb.astype(jnp.float32), axis=2)
    top = min(MOBA_TOPK, n_blk)
    scale = hd ** -0.5
    R = B * T
    rc = min(MOBA_ROW_CHUNK, R)
    n_chunks = -(-R // rc)
    rpad = n_chunks * rc - R
    rows_q = jnp.pad(q.reshape(R, H, hd), ((0, rpad), (0, 0), (0, 0))).reshape(n_chunks, rc, H, hd)
    rows_b = jnp.pad(jnp.repeat(jnp.arange(B, dtype=jnp.int32), T), (0, rpad)).reshape(n_chunks, rc)
    rows_p = jnp.pad(jnp.tile(q_pos.astype(jnp.int32), B), (0, rpad)).reshape(n_chunks, rc)
    offs = jnp.arange(MOBA_BLOCK, dtype=jnp.int32)
    heads = jnp.arange(H)[None, :, None]

    def chunk(args):
        qc, bc, pc = args
        own = pc // MOBA_BLOCK
        gate = jnp.einsum('rhd,rnhd->rhn', qc.astype(jnp.float32), k_mean[bc])
        gate = jnp.where(jnp.arange(n_blk)[None, None, :] < own[:, None, None], gate, -jnp.inf)
        _, sel = lax.top_k(gate, top)
        idx = jnp.concatenate([sel.astype(jnp.int32), jnp.broadcast_to(own[:, None, None], (rc, H, 1))], axis=-1)
        slot_ok = jnp.concatenate([jnp.arange(top)[None, :] < own[:, None], jnp.ones((rc, 1), dtype=bool)], axis=-1)
        kg = kb[bc[:, None, None], idx, :, heads, :]
        vg = vb[bc[:, None, None], idx, :, heads, :]
        s = jnp.einsum('rhd,rhnjd->rhnj', qc, kg).astype(jnp.float32) * scale
        kpos = idx[..., None] * MOBA_BLOCK + offs
        ok = slot_ok[:, None, :, None] & (kpos <= pc[:, None, None, None])
        p = jax.nn.softmax(jnp.where(ok, s, -jnp.inf), axis=(-2, -1))
        return jnp.einsum('rhnj,rhnjd->rhd', p.astype(vg.dtype), vg)

    out = lax.map(chunk, (rows_q, rows_b, rows_p))
    return out.reshape(n_chunks * rc, H, hd)[:R].reshape(B, T, H, hd)


def moba_qkv(h, pos, w_qkv):
    B, T, _ = h.shape
    qkv = (h @ w_qkv).reshape(B, T, 3, MOBA_HEADS, MOBA_HD)
    return rope(qkv[:, :, 0], pos), rope(qkv[:, :, 1], pos), qkv[:, :, 2]


def moba_prompt(h, pos, w_qkv, w_o):
    B, T, _ = h.shape
    q, k, v = moba_qkv(h, pos, w_qkv)
    o = moba_attend(q, k, v, pos)
    return o.reshape(B, T, D_MODEL) @ w_o, k, v


def moba_sample(h, pos, w_qkv, w_o, k_pool, v_pool, page_table):
    B, T, _ = h.shape
    q, k, v = moba_qkv(h, pos, w_qkv)
    past_k = k_pool[page_table].reshape(B, -1, MOBA_HEADS, MOBA_HD).astype(k.dtype)
    past_v = v_pool[page_table].reshape(B, -1, MOBA_HEADS, MOBA_HD).astype(v.dtype)
    o = moba_attend(q, jnp.concatenate([past_k, k], axis=1), jnp.concatenate([past_v, v], axis=1), pos)
    return o.reshape(B, T, D_MODEL) @ w_o, k, v


def hgrn2_scan(q, k, log_f, v, s0):
    B, T, H, K = q.shape
    V = v.shape[-1]
    L = min(HGRN_CHUNK, T)
    n_c = -(-T // L)
    pad = n_c * L - T

    def chunks(a):
        a = jnp.pad(a.astype(jnp.float32), ((0, 0), (0, pad), (0, 0), (0, 0)))
        return a.reshape(B, n_c, L, H, a.shape[-1]).transpose(1, 0, 3, 2, 4)

    causal = jnp.tril(jnp.ones((L, L), dtype=bool))[:, :, None]

    def step(S, xs):
        qc, kc, fc, vc = xs
        G = jnp.cumsum(fc, axis=2)
        decay = jnp.exp(jnp.where(causal, G[:, :, :, None, :] - G[:, :, None, :, :], -jnp.inf))
        A = jnp.einsum('bhtk,bhsk,bhtsk->bhts', qc, kc, decay)
        o = jnp.einsum('bhts,bhsv->bhtv', A, vc) + jnp.einsum('bhtk,bhkv->bhtv', qc * jnp.exp(G), S)
        G_end = G[:, :, -1:, :]
        S = jnp.exp(G_end[:, :, 0, :])[..., None] * S + jnp.einsum('bhsk,bhsv->bhkv', kc * jnp.exp(G_end - G), vc)
        return S, o

    S_fin, o = lax.scan(step, s0.astype(jnp.float32), (chunks(q), chunks(k), chunks(log_f), chunks(v)))
    o = o.transpose(1, 0, 3, 2, 4).reshape(B, n_c * L, H, V)[:, :T]
    return o, S_fin


def hgrn2_mixer(h, w_in, w_o, norm_g, lb, s0):
    B, T, _ = h.shape
    q, f, i_in, g = jnp.split(h @ w_in, 4, axis=-1)
    f = f.astype(jnp.float32).reshape(B, T, HGRN_HEADS, HGRN_KD)
    lb = lb.reshape(HGRN_HEADS, HGRN_KD)
    log_f = jnp.logaddexp(jnp.log(lb), jnp.log1p(-lb) + jax.nn.log_sigmoid(f))
    k = (1.0 - lb) * jax.nn.sigmoid(-f)
    q = jax.nn.silu(q).reshape(B, T, HGRN_HEADS, HGRN_KD)
    v = i_in.reshape(B, T, HGRN_HEADS, HGRN_VD)
    o, s_fin = hgrn2_scan(q, k, log_f, v, s0)
    o = rms_norm(o.reshape(B, T, D_MODEL), norm_g) * jax.nn.silu(g.astype(jnp.float32))
    return o @ w_o, s_fin


def dilated_group_prompt(q, k, v, window, dil):
    B, S, H, hd = q.shape
    n = window // dil
    span = dil * n
    Sp = -(-S // span) * span
    nb = Sp // span

    def blocks(a):
        a = jnp.pad(a, ((0, 0), (0, Sp - S), (0, 0), (0, 0)))
        a = a.reshape(B, Sp // dil, dil, H, hd).transpose(0, 2, 1, 3, 4)
        return a.reshape(B, dil, nb, n, H, hd)

    def with_prev(a):
        prev = jnp.pad(a[:, :, :-1], ((0, 0), (0, 0), (1, 0), (0, 0), (0, 0), (0, 0)))
        return jnp.concatenate([prev, a], axis=3)

    qb = blocks(q)
    kk = with_prev(blocks(k))
    vv = with_prev(blocks(v))
    s = jnp.einsum('bdnqhe,bdnkhe->bdnhqk', qb, kk).astype(jnp.float32) * hd ** -0.5
    rel = jnp.arange(n)[:, None] + n - jnp.arange(2 * n)[None, :]
    band = (rel >= 0) & (rel <= n)
    has_prev = (jnp.arange(nb)[:, None, None] > 0) | (jnp.arange(2 * n)[None, None, :] >= n)
    mask = band[None] & has_prev
    s = jnp.where(mask[None, None, :, None], s, -jnp.inf)
    lse = jax.nn.logsumexp(s, axis=-1)
    p = jnp.exp(s - lse[..., None])
    o = jnp.einsum('bdnhqk,bdnkhe->bdnqhe', p, vv.astype(jnp.float32))
    o = o.reshape(B, dil, Sp // dil, H, hd).transpose(0, 2, 1, 3, 4).reshape(B, Sp, H, hd)[:, :S]
    lse = lse.transpose(0, 1, 2, 4, 3).reshape(B, dil, Sp // dil, H).transpose(0, 2, 1, 3).reshape(B, Sp, H)[:, :S]
    return o, lse


def dilated_group_sample(q, k_cat, v_cat, window, dil):
    B, T, H, hd = q.shape
    W = k_cat.shape[1] - T
    n = window // dil
    local = W + jnp.arange(T)[:, None] - dil * jnp.arange(n + 1)[None, :]
    ok = local >= 0
    idx = jnp.maximum(local, 0)
    kg = k_cat[:, idx]
    vg = v_cat[:, idx]
    s = jnp.einsum('bthe,btmhe->bthm', q, kg).astype(jnp.float32) * hd ** -0.5
    s = jnp.where(ok[None, :, None, :], s, -jnp.inf)
    lse = jax.nn.logsumexp(s, axis=-1)
    p = jnp.exp(s - lse[..., None])
    o = jnp.einsum('bthm,btmhe->bthe', p, vg.astype(jnp.float32))
    return o, lse


def dilated_mixer(h, pos, w_in, w_o, bufs):
    B, T, _ = h.shape
    proj = (h @ w_in).reshape(B, T, N_DIL_GROUPS, 3, DIL_HEADS, DIL_HD)
    outs, lses, new_bufs = [], [], []
    for g, (window, dil) in enumerate(DIL_PAIRS):
        q = rope(proj[:, :, g, 0], pos)
        k = rope(proj[:, :, g, 1], pos)
        v = proj[:, :, g, 2]
        if bufs is None:
            o, lse = dilated_group_prompt(q, k, v, window, dil)
            keep = min(window, T)
            new_bufs.append((k[:, T - keep:], v[:, T - keep:]))
        else:
            kb, vb = bufs[g]
            k_cat = jnp.concatenate([kb.astype(k.dtype), k], axis=1)
            v_cat = jnp.concatenate([vb.astype(v.dtype), v], axis=1)
            o, lse = dilated_group_sample(q, k_cat, v_cat, window, dil)
            keep = min(window, k_cat.shape[1])
            new_bufs.append((k_cat[:, -keep:], v_cat[:, -keep:]))
        outs.append(o)
        lses.append(lse)
    w = jax.nn.softmax(jnp.stack(lses, axis=0), axis=0)
    o = jnp.sum(w[..., None] * jnp.stack(outs, axis=0), axis=0)
    return o.reshape(B, T, DIL_WIDTH) @ w_o, new_bufs


def setup_inputs(seed: int = 0) -> dict:
    key = jax.random.key(seed)
    ks = jax.random.split(key, 40)
    D = D_MODEL
    n_pages = PAST_LEN // PAGE_SIZE
    n_pool = (DEC_BATCH * n_pages * 5) // 4

    def nrm(k, shape, scale=1.0):
        return jax.random.normal(k, shape, jnp.float32) * scale

    page_table = jax.random.permutation(ks[4], n_pool)[: DEC_BATCH * n_pages].reshape(DEC_BATCH, n_pages).astype(jnp.int32)
    dil_len = [min(w, PAST_LEN) for w, _ in DIL_PAIRS]
    return {
        'x_prompt': nrm(ks[0], (BATCH, SEQ, D)),
        'x_sample': nrm(ks[1], (DEC_BATCH, DEC_SEQ, D)),
        'cache_k_moba': nrm(ks[2], (N_MOBA_LAYERS, n_pool, PAGE_SIZE, MOBA_HEADS, MOBA_HD)),
        'cache_v_moba': nrm(ks[3], (N_MOBA_LAYERS, n_pool, PAGE_SIZE, MOBA_HEADS, MOBA_HD)),
        'page_table': page_table,
        'state_hgrn': nrm(ks[5], (N_HGRN_LAYERS, DEC_BATCH, HGRN_HEADS, HGRN_KD, HGRN_VD), 0.5),
        'cache_k_dil0': nrm(ks[6], (N_DIL_LAYERS, DEC_BATCH, dil_len[0], DIL_HEADS, DIL_HD)),
        'cache_v_dil0': nrm(ks[7], (N_DIL_LAYERS, DEC_BATCH, dil_len[0], DIL_HEADS, DIL_HD)),
        'cache_k_dil1': nrm(ks[8], (N_DIL_LAYERS, DEC_BATCH, dil_len[1], DIL_HEADS, DIL_HD)),
        'cache_v_dil1': nrm(ks[9], (N_DIL_LAYERS, DEC_BATCH, dil_len[1], DIL_HEADS, DIL_HD)),
        'cache_k_dil2': nrm(ks[10], (N_DIL_LAYERS, DEC_BATCH, dil_len[2], DIL_HEADS, DIL_HD)),
        'cache_v_dil2': nrm(ks[11], (N_DIL_LAYERS, DEC_BATCH, dil_len[2], DIL_HEADS, DIL_HD)),
        'c_prompt': nrm(ks[12], (BATCH, D)),
        'c_sample': nrm(ks[13], (DEC_BATCH, D)),
        'norm_mix_g': 1.0 + nrm(ks[14], (DEPTH, D), 0.05),
        'norm_ffn_g': 1.0 + nrm(ks[15], (DEPTH, D), 0.05),
        'norm_final_g': 1.0 + nrm(ks[16], (D,), 0.05),
        'ada_w': nrm(ks[17], (DEPTH, D, 6 * D), 0.5 * D ** -0.5),
        'ada_b': nrm(ks[18], (DEPTH, 6 * D), 0.1),
        'moba_w_qkv': nrm(ks[19], (N_MOBA_LAYERS, D, 3 * D), D ** -0.5),
        'moba_w_o': nrm(ks[20], (N_MOBA_LAYERS, D, D), D ** -0.5),
        'hgrn_w_in': nrm(ks[21], (N_HGRN_LAYERS, D, 4 * D), D ** -0.5),
        'hgrn_w_o': nrm(ks[22], (N_HGRN_LAYERS, D, D), D ** -0.5),
        'hgrn_norm_g': 1.0 + nrm(ks[23], (N_HGRN_LAYERS, D), 0.05),
        'hgrn_lower_bounds': nrm(ks[24], (DEPTH, D), 0.5),
        'dil_w_in': nrm(ks[25], (N_DIL_LAYERS, D, N_DIL_GROUPS * 3 * DIL_WIDTH), D ** -0.5),
        'dil_w_o': nrm(ks[26], (N_DIL_LAYERS, DIL_WIDTH, D), DIL_WIDTH ** -0.5),
        'ffn_w1': nrm(ks[27], (DEPTH, D, D_FF), D ** -0.5),
        'ffn_w2': nrm(ks[28], (DEPTH, D_FF, D), D_FF ** -0.5),
    }


def reference(x_prompt, x_sample, cache_k_moba, cache_v_moba, page_table, state_hgrn,
              cache_k_dil0, cache_v_dil0, cache_k_dil1, cache_v_dil1, cache_k_dil2, cache_v_dil2,
              c_prompt, c_sample, norm_mix_g, norm_ffn_g, norm_final_g, ada_w, ada_b,
              moba_w_qkv, moba_w_o, hgrn_w_in, hgrn_w_o, hgrn_norm_g, hgrn_lower_bounds,
              dil_w_in, dil_w_o, ffn_w1, ffn_w2):
    B_p, S, _ = x_prompt.shape
    T = x_sample.shape[1]
    past = page_table.shape[1] * PAGE_SIZE
    pos_p = jnp.arange(S, dtype=jnp.int32)
    pos_s = past + jnp.arange(T, dtype=jnp.int32)
    lb_all = jnp.cumsum(jax.nn.softmax(hgrn_lower_bounds.astype(jnp.float32), axis=0), axis=0)
    lb_all = lb_all - lb_all[0:1]
    dil_caches = ((cache_k_dil0, cache_v_dil0), (cache_k_dil1, cache_v_dil1), (cache_k_dil2, cache_v_dil2))

    moba_kp, moba_vp, moba_ks, moba_vs = [], [], [], []
    hgrn_sp, hgrn_ss = [], []
    dil_kp = [[] for _ in range(N_DIL_GROUPS)]
    dil_vp = [[] for _ in range(N_DIL_GROUPS)]
    dil_ks = [[] for _ in range(N_DIL_GROUPS)]
    dil_vs = [[] for _ in range(N_DIL_GROUPS)]

    xp, xs = x_prompt, x_sample
    for i in range(DEPTH):
        kind, j = i % N_MIXERS, i // N_MIXERS
        sh1p, sc1p, g1p, sh2p, sc2p, g2p = modulation(c_prompt, ada_w[i], ada_b[i])
        sh1s, sc1s, g1s, sh2s, sc2s, g2s = modulation(c_sample, ada_w[i], ada_b[i])
        hp = rms_norm(xp, norm_mix_g[i]) * (1.0 + sc1p) + sh1p
        hs = rms_norm(xs, norm_mix_g[i]) * (1.0 + sc1s) + sh1s
        if kind == 0:
            mp, kp, vp = moba_prompt(hp, pos_p, moba_w_qkv[j], moba_w_o[j])
            ms, ks_, vs_ = moba_sample(hs, pos_s, moba_w_qkv[j], moba_w_o[j], cache_k_moba[j], cache_v_moba[j], page_table)
            moba_kp.append(kp)
            moba_vp.append(vp)
            moba_ks.append(ks_)
            moba_vs.append(vs_)
        elif kind == 1:
            s0 = jnp.zeros((B_p, HGRN_HEADS, HGRN_KD, HGRN_VD), jnp.float32)
            mp, sp = hgrn2_mixer(hp, hgrn_w_in[j], hgrn_w_o[j], hgrn_norm_g[j], lb_all[i], s0)
            ms, ss = hgrn2_mixer(hs, hgrn_w_in[j], hgrn_w_o[j], hgrn_norm_g[j], lb_all[i], state_hgrn[j])
            hgrn_sp.append(sp)
            hgrn_ss.append(ss)
        else:
            mp, bufp = dilated_mixer(hp, pos_p, dil_w_in[j], dil_w_o[j], None)
            ms, bufs = dilated_mixer(hs, pos_s, dil_w_in[j], dil_w_o[j], [(ck[j], cv[j]) for ck, cv in dil_caches])
            for g in range(N_DIL_GROUPS):
                dil_kp[g].append(bufp[g][0])
                dil_vp[g].append(bufp[g][1])
                dil_ks[g].append(bufs[g][0])
                dil_vs[g].append(bufs[g][1])
        xp = xp + (g1p * mp).astype(xp.dtype)
        xs = xs + (g1s * ms).astype(xs.dtype)
        hp = rms_norm(xp, norm_ffn_g[i]) * (1.0 + sc2p) + sh2p
        hs = rms_norm(xs, norm_ffn_g[i]) * (1.0 + sc2s) + sh2s
        xp = xp + (g2p * sq_relu_mlp(hp, ffn_w1[i], ffn_w2[i])).astype(xp.dtype)
        xs = xs + (g2s * sq_relu_mlp(hs, ffn_w1[i], ffn_w2[i])).astype(xs.dtype)

    y_prompt = rms_norm(xp, norm_final_g)
    y_sample = rms_norm(xs, norm_final_g)
    moba_k_prompt = jnp.stack(moba_kp)
    moba_v_prompt = jnp.stack(moba_vp)
    moba_k_sample = jnp.stack(moba_ks)
    moba_v_sample = jnp.stack(moba_vs)
    hgrn_state_prompt = jnp.stack(hgrn_sp)
    hgrn_state_sample = jnp.stack(hgrn_ss)
    dil0_k_prompt = jnp.stack(dil_kp[0])
    dil0_v_prompt = jnp.stack(dil_vp[0])
    dil1_k_prompt = jnp.stack(dil_kp[1])
    dil1_v_prompt = jnp.stack(dil_vp[1])
    dil2_k_prompt = jnp.stack(dil_kp[2])
    dil2_v_prompt = jnp.stack(dil_vp[2])
    dil0_k_sample = jnp.stack(dil_ks[0])
    dil0_v_sample = jnp.stack(dil_vs[0])
    dil1_k_sample = jnp.stack(dil_ks[1])
    dil1_v_sample = jnp.stack(dil_vs[1])
    dil2_k_sample = jnp.stack(dil_ks[2])
    dil2_v_sample = jnp.stack(dil_vs[2])
    return (y_prompt, y_sample, moba_k_prompt, moba_v_prompt, moba_k_sample, moba_v_sample,
            hgrn_state_prompt, hgrn_state_sample,
            dil0_k_prompt, dil0_v_prompt, dil1_k_prompt, dil1_v_prompt, dil2_k_prompt, dil2_v_prompt,
            dil0_k_sample, dil0_v_sample, dil1_k_sample, dil1_v_sample, dil2_k_sample, dil2_v_sample)
```

```python
import functools

import jax
import jax.numpy as jnp
from jax import lax
from jax.experimental import pallas as pl
from jax.experimental.pallas import tpu as pltpu

F32 = jnp.float32
BF16 = jnp.bfloat16
NEG_INF = float("-inf")

D = 1024
LANES = 128
SUBLANES = 8
BF16_ROWS = 16
NORM_EPS = 1e-6
ROPE_THETA = 10000.0
PAGE = 128
MOBA_HD = 64
MOBA_H = D // MOBA_HD
MOBA_BLOCK = 256
MOBA_TOPK = 3
HG_H = 8
DIL = (1, 4, 16)
DIL_N = 128
DIL_H = 8
DIL_SUPER = DIL_N * DIL[-1]
VMEM_LIMIT = 56 * 1024 * 1024

NT = (((1,), (1,)), ((), ()))
HIGHEST = lax.Precision.HIGHEST


def _cparams(n_axes):
    return pltpu.CompilerParams(dimension_semantics=("arbitrary",) * n_axes, vmem_limit_bytes=VMEM_LIMIT)


def _resident(shape):
    zeros = (0,) * len(shape)
    return pl.BlockSpec(shape, lambda *_: zeros, pipeline_mode=pl.Buffered(1))


def _silu(x):
    return x * jax.nn.sigmoid(x)


def _rms(x, g):
    return x * lax.rsqrt(jnp.mean(x * x, axis=-1, keepdims=True) + NORM_EPS) * g


def _norm_mod(x, g, sc, sh):
    return _rms(x, g) * (1.0 + sc) + sh


def _ada_kernel(c_ref, w_ref, b_ref, o_ref):
    s = _silu(c_ref[...]).astype(BF16)
    o_ref[...] = jnp.dot(s, w_ref[...].astype(BF16), preferred_element_type=F32) + b_ref[...]


def ada_modulation(c_all, ada_w, ada_b):
    n_layers, _, n = ada_w.shape
    m = c_all.shape[0]
    tn = 1536
    return pl.pallas_call(
        _ada_kernel,
        grid=(n_layers, n // tn),
        in_specs=[pl.BlockSpec((m, D), lambda l, j: (0, 0)),
                  pl.BlockSpec((None, D, tn), lambda l, j: (l, 0, j)),
                  pl.BlockSpec((None, 1, tn), lambda l, j: (l, 0, j))],
        out_specs=pl.BlockSpec((None, m, tn), lambda l, j: (l, 0, j)),
        out_shape=jax.ShapeDtypeStruct((n_layers, m, n), F32),
        compiler_params=_cparams(2), name="ada_mod",
    )(c_all, ada_w, ada_b.reshape(n_layers, 1, n))


class Stream:
    def __init__(self, rows, tm, rows_per_batch, per_row):
        self.rows, self.tm, self.rows_per_batch, self.per_row = rows, tm, rows_per_batch, per_row
        self.n_tiles = rows // tm
        self.tiles_per_batch = max(rows_per_batch // tm, 1)

    def row_spec(self, width=D):
        return pl.BlockSpec((self.tm, width), lambda i: (i, 0))

    def mod_spec(self, chunk):
        if self.per_row:
            return pl.BlockSpec((self.tm, D), lambda i: (i, chunk))
        tpb = self.tiles_per_batch
        return pl.BlockSpec((None, 1, D), lambda i: (i // tpb, 0, chunk))


def _vec_spec():
    return pl.BlockSpec((1, D), lambda i: (0, 0))


def _rope_tables(pos, hd):
    inv = ROPE_THETA ** (-jnp.arange(0, hd, 2, dtype=F32) / hd)
    ang = pos.astype(F32)[:, None] * inv[None, :]
    return jnp.cos(ang), jnp.sin(ang)


def _rope_row_tables(pos, hd):
    cos, sin = _rope_tables(pos, hd)
    rep = LANES // hd
    return (jnp.tile(jnp.concatenate([cos, cos], axis=1), (1, rep)),
            jnp.tile(jnp.concatenate([-sin, sin], axis=1), (1, rep)))


def _rope_rows(x, cos, sin, hd):
    lane = lax.broadcasted_iota(jnp.int32, (1, LANES), 1)
    first_half = (lane % hd) < (hd // 2)
    outs = []
    for c in range(x.shape[1] // LANES):
        xc = x[:, c * LANES:(c + 1) * LANES]
        if hd == LANES:
            rot = pltpu.roll(xc, hd // 2, 1)
        else:
            rot = jnp.where(first_half, pltpu.roll(xc, LANES - hd // 2, 1), pltpu.roll(xc, hd // 2, 1))
        outs.append(xc * cos + rot * sin)
    return jnp.concatenate(outs, axis=1)


def _rope_proj_kernel(hd, x_ref, g_ref, sc_ref, sh_ref, w_ref, cos_ref, sin_ref, q_ref, k_ref, v_ref):
    h = _norm_mod(x_ref[...], g_ref[...], sc_ref[...], sh_ref[...]).astype(BF16)
    qkv = jnp.dot(h, w_ref[...], preferred_element_type=F32)
    cos, sin = cos_ref[...], sin_ref[...]
    q_ref[...] = _rope_rows(qkv[:, :D], cos, sin, hd).astype(q_ref.dtype)
    k_ref[...] = _rope_rows(qkv[:, D:2 * D], cos, sin, hd)
    v_ref[...] = qkv[:, 2 * D:]


def rope_proj(st, x, norm_g, mod, w, cos, sin, hd, q_dtype):
    n_groups = w.shape[0]
    n_tab = cos.shape[0] // st.tm
    row = lambda g, i: (i, 0)
    mod_s = lambda chunk: (pl.BlockSpec((st.tm, D), lambda g, i: (i, chunk)) if st.per_row else
                           pl.BlockSpec((None, 1, D), lambda g, i: (i // st.tiles_per_batch, 0, chunk)))
    tab = pl.BlockSpec((st.tm, LANES), lambda g, i: (i % n_tab, 0))
    out = pl.BlockSpec((None, st.tm, D), lambda g, i: (g, i, 0))
    return pl.pallas_call(
        functools.partial(_rope_proj_kernel, hd),
        grid=(n_groups, st.n_tiles),
        in_specs=[pl.BlockSpec((st.tm, D), row), pl.BlockSpec((1, D), lambda g, i: (0, 0)), mod_s(1), mod_s(0),
                  pl.BlockSpec((None, D, 3 * D), lambda g, i: (g, 0, 0)), tab, tab],
        out_specs=[out, out, out],
        out_shape=[jax.ShapeDtypeStruct((n_groups, st.rows, D), q_dtype),
                   jax.ShapeDtypeStruct((n_groups, st.rows, D), F32),
                   jax.ShapeDtypeStruct((n_groups, st.rows, D), F32)],
        compiler_params=_cparams(2), name="rope_proj",
    )(x, norm_g, mod, mod, w, cos, sin)


def _moba_prompt_proj_kernel(x_ref, g_ref, sc_ref, sh_ref, wq_ref, wkvt_ref, cr_ref, sr_ref, ct_ref, st_ref,
                             q_ref, kt_ref, vt_ref):
    h = _norm_mod(x_ref[...], g_ref[...], sc_ref[...], sh_ref[...]).astype(BF16)
    q = jnp.dot(h, wq_ref[...], preferred_element_type=F32)
    q_ref[...] = _rope_rows(q, cr_ref[...], sr_ref[...], MOBA_HD).astype(BF16)
    kvt = lax.dot_general(wkvt_ref[...], h, NT, preferred_element_type=F32)
    ct, st = ct_ref[...], st_ref[...]
    half = MOBA_HD // 2
    for hh in range(MOBA_H):
        lo = hh * MOBA_HD
        x1, x2 = kvt[lo:lo + half], kvt[lo + half:lo + MOBA_HD]
        kt_ref[lo:lo + half, :] = x1 * ct - x2 * st
        kt_ref[lo + half:lo + MOBA_HD, :] = x2 * ct + x1 * st
    vt_ref[...] = kvt[D:]


def moba_prompt_proj(st, x, norm_g, mod, wq, wkvt, pos, batch, seq):
    cos_r, sin_r = _rope_row_tables(pos, MOBA_HD)
    cos, sin = _rope_tables(pos, MOBA_HD)
    tpb = st.tiles_per_batch
    tab_r = pl.BlockSpec((st.tm, LANES), lambda i: (i % tpb, 0))
    tab_t = pl.BlockSpec((MOBA_HD // 2, st.tm), lambda i: (0, i % tpb))
    out_t = pl.BlockSpec((None, D, st.tm), lambda i: (i // tpb, 0, i % tpb))
    return pl.pallas_call(
        _moba_prompt_proj_kernel,
        grid=(st.n_tiles,),
        in_specs=[st.row_spec(), _vec_spec(), st.mod_spec(1), st.mod_spec(0), _resident((D, D)),
                  _resident((2 * D, D)), tab_r, tab_r, tab_t, tab_t],
        out_specs=[st.row_spec(), out_t, out_t],
        out_shape=[jax.ShapeDtypeStruct((st.rows, D), BF16),
                   jax.ShapeDtypeStruct((batch, D, seq), F32),
                   jax.ShapeDtypeStruct((batch, D, seq), F32)],
        compiler_params=_cparams(1), name="moba_prompt_proj",
    )(x, norm_g, mod, mod, wq, wkvt, cos_r, sin_r, cos.T, sin.T)


def _hgrn_proj_kernel(x_ref, g_ref, sc_ref, sh_ref, w_ref, loglb_ref, log1mlb_ref, omlb_ref,
                      q_ref, lf_ref, k_ref, v_ref, gs_ref):
    h = _norm_mod(x_ref[...], g_ref[...], sc_ref[...], sh_ref[...]).astype(BF16)
    z = jnp.dot(h, w_ref[...], preferred_element_type=F32)
    q_ref[...] = _silu(z[:, :D])
    f = z[:, D:2 * D]
    log_sig = jnp.minimum(f, 0.0) - jnp.log1p(jnp.exp(-jnp.abs(f)))
    a = loglb_ref[...]
    b = log1mlb_ref[...] + log_sig
    lf_ref[...] = jnp.maximum(a, b) + jnp.log1p(jnp.exp(-jnp.abs(a - b)))
    k_ref[...] = omlb_ref[...] * jax.nn.sigmoid(-f)
    v_ref[...] = z[:, 2 * D:3 * D]
    gs_ref[...] = _silu(z[:, 3 * D:])


def hgrn_proj(st, x, norm_g, mod, w, lb):
    lb = lb.reshape(1, D)
    outs = [jax.ShapeDtypeStruct((st.rows, D), F32)] * 5
    return pl.pallas_call(
        _hgrn_proj_kernel,
        grid=(st.n_tiles,),
        in_specs=[st.row_spec(), _vec_spec(), st.mod_spec(1), st.mod_spec(0), _resident((D, 4 * D)),
                  _vec_spec(), _vec_spec(), _vec_spec()],
        out_specs=[st.row_spec()] * 5,
        out_shape=outs,
        compiler_params=_cparams(1), name="hgrn_proj",
    )(x, norm_g, mod, mod, w, jnp.log(lb), jnp.log1p(-lb), 1.0 - lb)


def _post_kernel(hgrn, final, *refs):
    refs = list(refs)
    o_ref = refs.pop(0)
    if hgrn:
        gs_ref, ng_ref = refs.pop(0), refs.pop(0)
    (x_ref, g1_ref, wo_ref, nf_ref, sc_ref, sh_ref, g2_ref, w1_ref, w2_ref) = refs[:9]
    refs = refs[9:]
    if final:
        fin_ref = refs.pop(0)
    out_ref = refs.pop(0)

    o = o_ref[...]
    if hgrn:
        o = _rms(o, ng_ref[...]) * gs_ref[...]
    x = x_ref[...] + g1_ref[...] * jnp.dot(o.astype(BF16), wo_ref[...], preferred_element_type=F32)
    h = _norm_mod(x, nf_ref[...], sc_ref[...], sh_ref[...]).astype(BF16)
    acc = x
    d_ff = w1_ref.shape[1]
    chunk = D
    y = None
    for c in range(d_ff // chunk):
        u = jnp.dot(h, w1_ref[:, c * chunk:(c + 1) * chunk], preferred_element_type=F32)
        u = jnp.maximum(u, 0.0)
        u = (u * u).astype(BF16)
        part = jnp.dot(u, w2_ref[c * chunk:(c + 1) * chunk, :], preferred_element_type=F32)
        y = part if y is None else y + part
    acc = acc + g2_ref[...] * y
    if final:
        acc = _rms(acc, fin_ref[...])
    out_ref[...] = acc


def post_mixer(st, o, x, mod, wo, norm_ffn_g, w1, w2, hgrn_extra=None, final_g=None):
    hgrn, final = hgrn_extra is not None, final_g is not None
    args, specs = [o], [st.row_spec()]
    if hgrn:
        args += list(hgrn_extra)
        specs += [st.row_spec(), _vec_spec()]
    args += [x, mod, wo, norm_ffn_g, mod, mod, mod, w1, w2]
    specs += [st.row_spec(), st.mod_spec(2), _resident(wo.shape), _vec_spec(), st.mod_spec(4), st.mod_spec(3),
              st.mod_spec(5), _resident(w1.shape), _resident(w2.shape)]
    if final:
        args.append(final_g)
        specs.append(_vec_spec())
    return pl.pallas_call(
        functools.partial(_post_kernel, hgrn, final),
        grid=(st.n_tiles,),
        in_specs=specs,
        out_specs=st.row_spec(),
        out_shape=jax.ShapeDtypeStruct((st.rows, D), F32),
        compiler_params=_cparams(1), name="post_mixer",
    )(*args)


def _top_blocks(gate, idx, axis):
    sel = jnp.zeros(gate.shape, jnp.bool_)
    for _ in range(MOBA_TOPK):
        m = jnp.max(gate, axis=axis, keepdims=True)
        first = jnp.min(jnp.where(gate == m, idx, float(2 ** 20)), axis=axis, keepdims=True)
        hit = idx == first
        sel = sel | (hit & (m > NEG_INF))
        gate = jnp.where(hit, NEG_INF, gate)
    return sel


def _moba_prompt_kernel(n_blk, q_ref, kt_ref, vt_ref, o_ref, kb_ref, vb_ref, km_ref, bias_ref):
    i = pl.program_id(2)
    blk = MOBA_BLOCK

    @pl.when(i == 0)
    def _():
        for n in range(n_blk):
            kn = kt_ref[:, n * blk:(n + 1) * blk].T
            kb_ref[n] = kn.astype(BF16)
            km_ref[pl.ds(n, 1), :] = jnp.mean(kn, axis=0, keepdims=True)
            vb_ref[n] = vt_ref[:, n * blk:(n + 1) * blk].astype(BF16)

    q2 = q_ref[...]
    lane = lax.broadcasted_iota(jnp.int32, (1, LANES), 1)
    blk_id = lax.broadcasted_iota(jnp.int32, (n_blk, blk), 0)
    key_i = lax.broadcasted_iota(jnp.int32, (blk, blk), 0)
    qry_i = lax.broadcasted_iota(jnp.int32, (blk, blk), 1)
    scale = MOBA_HD ** -0.5
    outs = []
    for a in range(LANES // MOBA_HD):
        qa = jnp.where((lane >= a * MOBA_HD) & (lane < (a + 1) * MOBA_HD), q2, jnp.zeros_like(q2))
        gate = lax.dot_general(km_ref[...], qa.astype(F32), NT, precision=HIGHEST, preferred_element_type=F32)
        gate = jnp.where(blk_id < i, gate, NEG_INF)
        sel = _top_blocks(gate, blk_id.astype(F32), 0)
        bias_ref[a] = jnp.where(sel, 0.0, NEG_INF)

        def scores(j):
            return lax.dot_general(kb_ref[j], qa, NT, preferred_element_type=F32) * scale

        v_lo = a * MOBA_HD
        s = jnp.where(key_i <= qry_i, scores(i), NEG_INF)
        m = jnp.max(s, axis=0, keepdims=True)
        p = jnp.exp(s - m)
        l = jnp.sum(p, axis=0, keepdims=True)
        acc = jnp.dot(vb_ref[i, v_lo:v_lo + MOBA_HD, :], p.astype(BF16), preferred_element_type=F32)

        def body(j, carry):
            m, l, acc = carry
            s = scores(j) + bias_ref[a, pl.ds(j, 1), :]
            m_new = jnp.maximum(m, jnp.max(s, axis=0, keepdims=True))
            alpha = jnp.exp(m - m_new)
            p = jnp.exp(s - m_new)
            l = alpha * l + jnp.sum(p, axis=0, keepdims=True)
            acc = alpha * acc + jnp.dot(vb_ref[j, v_lo:v_lo + MOBA_HD, :], p.astype(BF16),
                                        preferred_element_type=F32)
            return m_new, l, acc

        m, l, acc = lax.fori_loop(0, i, body, (m, l, acc))
        outs.append(acc / l)
    o_ref[...] = jnp.concatenate(outs, axis=0).T.astype(o_ref.dtype)


def moba_prompt_attention(q, kt, vt):
    batch, seq, _ = q.shape
    n_blk = seq // MOBA_BLOCK
    kv_spec = pl.BlockSpec((None, LANES, seq), lambda b, hp, i: (b, hp, 0))
    q_spec = pl.BlockSpec((None, MOBA_BLOCK, LANES), lambda b, hp, i: (b, i, hp))
    return pl.pallas_call(
        functools.partial(_moba_prompt_kernel, n_blk),
        grid=(batch, D // LANES, n_blk),
        in_specs=[q_spec, kv_spec, kv_spec],
        out_specs=q_spec,
        out_shape=jax.ShapeDtypeStruct((batch, seq, D), BF16),
        scratch_shapes=[pltpu.VMEM((n_blk, MOBA_BLOCK, LANES), BF16),
                        pltpu.VMEM((n_blk, LANES, MOBA_BLOCK), BF16),
                        pltpu.VMEM((n_blk, LANES), F32),
                        pltpu.VMEM((LANES // MOBA_HD, n_blk, MOBA_BLOCK), F32)],
        compiler_params=_cparams(3), name="moba_prompt_attn",
    )(q, kt, vt)


def _moba_sample_kernel(n_pages, dec, pt_ref, q_ref, k_ref, v_ref, *refs):
    del pt_ref
    k_pages, v_pages, o_ref = refs[:n_pages], refs[n_pages:2 * n_pages], refs[2 * n_pages]
    rows = MOBA_H * dec
    new_pad = -(-dec // BF16_ROWS) * BF16_ROWS
    pages_per_blk = MOBA_BLOCK // PAGE
    n_past = n_pages // pages_per_blk
    scale = MOBA_HD ** -0.5

    q = q_ref[...]
    row_h = lax.broadcasted_iota(jnp.int32, (rows, D), 0) // dec
    col_h = lax.broadcasted_iota(jnp.int32, (rows, D), 1) // MOBA_HD
    q_rep = jnp.broadcast_to(q[None], (MOBA_H, dec, D)).reshape(rows, D)
    q_bd = jnp.where(row_h == col_h, q_rep, 0.0).astype(BF16)

    s_pages = [jnp.dot(q_bd, kp[...].astype(BF16), preferred_element_type=F32) for kp in k_pages]
    lane = lax.broadcasted_iota(jnp.int32, (rows, LANES), 1)
    gate = jnp.full((rows, LANES), NEG_INF, F32)
    for n in range(n_past):
        tot = s_pages[n * pages_per_blk]
        for pg in range(1, pages_per_blk):
            tot = tot + s_pages[n * pages_per_blk + pg]
        gate = jnp.where(lane == n, jnp.sum(tot, axis=1, keepdims=True) * (1.0 / MOBA_BLOCK), gate)
    sel = _top_blocks(gate, lane.astype(F32), 1)
    bias = jnp.where(sel, 0.0, NEG_INF)

    s_past = [s_pages[pg] * scale + bias[:, pg // pages_per_blk:pg // pages_per_blk + 1] for pg in range(n_pages)]
    pad = jnp.zeros((new_pad - dec, D), F32)
    k_new = jnp.concatenate([k_ref[...], pad], axis=0).astype(BF16)
    v_new = jnp.concatenate([v_ref[...], pad], axis=0).astype(BF16)
    s_new = lax.dot_general(q_bd, k_new, NT, preferred_element_type=F32) * scale
    t_q = lax.broadcasted_iota(jnp.int32, (rows, new_pad), 0) % dec
    t_k = lax.broadcasted_iota(jnp.int32, (rows, new_pad), 1)
    s_new = jnp.where(t_k <= t_q, s_new, NEG_INF)

    m = jnp.max(s_new, axis=1, keepdims=True)
    for s in s_past:
        m = jnp.maximum(m, jnp.max(s, axis=1, keepdims=True))
    p_new = jnp.exp(s_new - m)
    l = jnp.sum(p_new, axis=1, keepdims=True)
    acc = jnp.dot(p_new.astype(BF16), v_new, preferred_element_type=F32)
    for s, vp in zip(s_past, v_pages):
        p = jnp.exp(s - m)
        l = l + jnp.sum(p, axis=1, keepdims=True)
        acc = acc + lax.dot_general(p.astype(BF16), vp[...].astype(BF16), NT, preferred_element_type=F32)
    acc = acc / l
    lane = lax.broadcasted_iota(jnp.int32, (dec, LANES), 1)
    heads_per_vreg = LANES // MOBA_HD
    outs = []
    for c in range(D // LANES):
        piece = acc[(c * heads_per_vreg) * dec:(c * heads_per_vreg + 1) * dec, c * LANES:(c + 1) * LANES]
        for a in range(1, heads_per_vreg):
            h = c * heads_per_vreg + a
            piece = jnp.where(lane >= a * MOBA_HD, acc[h * dec:(h + 1) * dec, c * LANES:(c + 1) * LANES], piece)
        outs.append(piece)
    o_ref[...] = jnp.concatenate(outs, axis=1).astype(o_ref.dtype)


def moba_sample_attention(q, k, v, k_cache_t, v_cache_t, layer, page_table, dec):
    batch, n_pages = page_table.shape
    new_spec = pl.BlockSpec((dec, D), lambda b, pt: (b, 0))

    def page_spec(pg):
        return pl.BlockSpec((None, None, D, PAGE), lambda b, pt: (layer, pt[b, pg], 0, 0))

    grid_spec = pltpu.PrefetchScalarGridSpec(
        num_scalar_prefetch=1, grid=(batch,),
        in_specs=[new_spec, new_spec, new_spec] + [page_spec(pg) for pg in range(n_pages)] * 2,
        out_specs=new_spec)
    return pl.pallas_call(
        functools.partial(_moba_sample_kernel, n_pages, dec),
        grid_spec=grid_spec,
        out_shape=jax.ShapeDtypeStruct((batch * dec, D), F32),
        compiler_params=_cparams(1), name="moba_sample_attn",
    )(page_table, q, k, v, *([k_cache_t] * n_pages), *([v_cache_t] * n_pages))


def _cumsum_rows(x):
    row = lax.broadcasted_iota(jnp.int32, x.shape, 0)
    shift = 1
    while shift < x.shape[0]:
        x = x + jnp.where(row >= shift, pltpu.roll(x, shift, 0), 0.0)
        shift *= 2
    return x


def _hgrn_kernel(carry, n_sub, *refs):
    if carry:
        q_ref, lf_ref, k_ref, v_ref, o_ref, s_out_ref, st_ref = refs
    else:
        q_ref, lf_ref, k_ref, v_ref, s_in_ref, o_ref, s_out_ref = refs
    c = pl.program_id(2)
    sub = SUBLANES
    vt = v_ref[...].T
    row = lax.broadcasted_iota(jnp.int32, (sub, LANES), 0)

    if carry:
        @pl.when(c == 0)
        def _():
            st_ref[...] = jnp.zeros_like(st_ref)

    for s in range(n_sub):
        sl = slice(s * sub, (s + 1) * sub)
        q, k, v = q_ref[sl, :], k_ref[sl, :], v_ref[sl, :]
        g = _cumsum_rows(lf_ref[sl, :])
        state_t = st_ref[...] if carry else s_in_ref[s].T
        o = lax.dot_general(q * jnp.exp(g), state_t, NT, preferred_element_type=F32)
        for d in range(sub):
            if d == 0:
                x = q * k
            else:
                x = q * pltpu.roll(k, d, 0) * jnp.exp(jnp.where(row >= d, g - pltpu.roll(g, d, 0), 0.0))
                x = jnp.where(row >= d, x, 0.0)
            a = jnp.sum(x, axis=1, keepdims=True)
            o = o + a * (v if d == 0 else pltpu.roll(v, d, 0))
        o_ref[sl, :] = o
        g_end = g[sub - 1:sub, :]
        k_dec = k * jnp.exp(g_end - g)
        state_t = state_t * jnp.exp(g_end) + jnp.dot(vt[:, sl], k_dec, preferred_element_type=F32)
        if carry:
            st_ref[...] = state_t
        else:
            s_out_ref[s] = state_t.T

    if carry:
        @pl.when(c == pl.num_programs(2) - 1)
        def _():
            s_out_ref[...] = st_ref[...].T


def hgrn_scan(q, lf, k, v, state0, batch, seq):
    rows = batch * seq
    tc = LANES
    state_shape = jax.ShapeDtypeStruct((batch, HG_H, LANES, LANES), F32)
    if state0 is None:
        n_chunks = seq // tc
        tile = pl.BlockSpec((tc, LANES), lambda b, h, c: (b * n_chunks + c, h))
        return pl.pallas_call(
            functools.partial(_hgrn_kernel, True, tc // SUBLANES),
            grid=(batch, HG_H, n_chunks),
            in_specs=[tile] * 4,
            out_specs=[tile, pl.BlockSpec((None, None, LANES, LANES), lambda b, h, c: (b, h, 0, 0))],
            out_shape=[jax.ShapeDtypeStruct((rows, D), F32), state_shape],
            scratch_shapes=[pltpu.VMEM((LANES, LANES), F32)],
            compiler_params=_cparams(3), name="hgrn_scan_prompt",
        )(q, lf, k, v)
    n_sub = tc // seq
    tile = pl.BlockSpec((tc, LANES), lambda nb, h, c: (nb, h))
    st_spec = pl.BlockSpec((n_sub, None, LANES, LANES), lambda nb, h, c: (nb, h, 0, 0))
    return pl.pallas_call(
        functools.partial(_hgrn_kernel, False, n_sub),
        grid=(batch // n_sub, HG_H, 1),
        in_specs=[tile] * 4 + [st_spec],
        out_specs=[tile, st_spec],
        out_shape=[jax.ShapeDtypeStruct((rows, D), F32), state_shape],
        compiler_params=_cparams(3), name="hgrn_scan_sample",
    )(q, lf, k, v, state0)


def _dil_prompt_kernel(*refs):
    n_g = len(DIL)
    q_refs, k_refs, v_refs = refs[0:n_g], refs[n_g:2 * n_g], refs[2 * n_g:3 * n_g]
    kp_refs, vp_refs = refs[3 * n_g:4 * n_g], refs[4 * n_g:5 * n_g]
    o_ref = refs[5 * n_g]
    kf_refs, vf_refs = refs[5 * n_g + 1:6 * n_g + 1], refs[6 * n_g + 1:7 * n_g + 1]
    og_ref, lg_ref = refs[7 * n_g + 1], refs[7 * n_g + 2]
    j = pl.program_id(1)
    n = DIL_N
    scale = float(LANES) ** -0.5
    qi = lax.broadcasted_iota(jnp.int32, (n, 2 * n), 0)
    kj = lax.broadcasted_iota(jnp.int32, (n, 2 * n), 1)
    rel = qi + n - kj
    band = (rel >= 0) & (rel <= n)

    for g, dil in enumerate(DIL):
        span = n * dil
        kf, vf = kf_refs[g], vf_refs[g]
        kf[0:span, :] = kp_refs[g][...]
        kf[span:span + DIL_SUPER, :] = k_refs[g][...]
        vf[0:span, :] = vp_refs[g][...]
        vf[span:span + DIL_SUPER, :] = v_refs[g][...]

        def unit(u, carry, g=g, dil=dil, span=span, kf=kf, vf=vf):
            start = (u // dil) * span + u % dil
            rows = lambda cnt: pl.ds(start, cnt) if dil == 1 else pl.ds(start, cnt, stride=dil)
            q = q_refs[g][rows(n), :].astype(BF16)
            kk = kf[rows(2 * n), :].astype(BF16)
            vv = vf[rows(2 * n), :].astype(BF16)
            s = lax.dot_general(q, kk, NT, preferred_element_type=F32) * scale
            first_key = jnp.where((j > 0) | (u >= dil), 0, n)
            s = jnp.where(band & (kj >= first_key), s, NEG_INF)
            m = jnp.max(s, axis=1, keepdims=True)
            p = jnp.exp(s - m)
            l = jnp.sum(p, axis=1, keepdims=True)
            o = jnp.dot(p.astype(BF16), vv, preferred_element_type=F32) / l
            og_ref[g, rows(n), :] = o
            lg_ref[g, rows(n), :] = jnp.broadcast_to(m + jnp.log(l), (n, LANES))
            return carry

        lax.fori_loop(0, DIL_SUPER // n, unit, 0)

    lse = [lg_ref[g] for g in range(n_g)]
    m = functools.reduce(jnp.maximum, lse)
    w = [jnp.exp(x - m) for x in lse]
    den = functools.reduce(lambda a, b: a + b, w)
    num = functools.reduce(lambda a, b: a + b, [w[g] * og_ref[g] for g in range(n_g)])
    o_ref[...] = (num / den).astype(o_ref.dtype)


def dil_prompt_attention(q, k, v, batch, seq):
    n_g = len(DIL)
    n_super = seq // DIL_SUPER
    cur, prev, scratch = [], [], []
    for g, dil in enumerate(DIL):
        span = DIL_N * dil
        per = DIL_SUPER // span
        cur.append(pl.BlockSpec((None, DIL_SUPER, LANES), lambda b, j, h, g=g: (g, b * n_super + j, h)))
        prev.append(pl.BlockSpec(
            (None, span, LANES),
            lambda b, j, h, g=g, per=per: (g, jnp.maximum((b * n_super + j) * per - 1, 0), h)))
        scratch.append(pltpu.VMEM((span + DIL_SUPER, LANES), F32))
    out_spec = pl.BlockSpec((DIL_SUPER, LANES), lambda b, j, h: (b * n_super + j, h))
    return pl.pallas_call(
        _dil_prompt_kernel,
        grid=(batch, n_super, DIL_H),
        in_specs=cur * 3 + prev * 2,
        out_specs=out_spec,
        out_shape=jax.ShapeDtypeStruct((batch * seq, D), BF16),
        scratch_shapes=scratch * 2 + [pltpu.VMEM((n_g, DIL_SUPER, LANES), F32)] * 2,
        compiler_params=_cparams(3), name="dil_prompt_attn",
    )(q, q, q, k, k, k, v, v, v, k, k, k, v, v, v)


def _stack_heads(x, n_heads):
    return jnp.concatenate([x[:, h * LANES:(h + 1) * LANES] for h in range(n_heads)], axis=0)


def _dil_sample_kernel(dec, q_ref, k_ref, v_ref, *refs):
    n_g = len(DIL)
    kc_refs, vc_refs, o_ref = refs[:n_g], refs[n_g:2 * n_g], refs[2 * n_g]
    rows = DIL_H * dec
    cols = DIL_N * DIL_H
    scale = float(LANES) ** -0.5
    r_i = lax.broadcasted_iota(jnp.int32, (rows, cols), 0)
    c_i = lax.broadcasted_iota(jnp.int32, (rows, cols), 1)
    same_head = (c_i % DIL_H) == (r_i // dec)
    t_row = r_i % dec
    pos_col = c_i // DIL_H
    rn = lax.broadcasted_iota(jnp.int32, (rows, rows), 0)
    cn = lax.broadcasted_iota(jnp.int32, (rows, rows), 1)
    new_head = (rn // dec) == (cn // dec)
    t_diff = rn % dec - cn % dec
    t_col1 = lax.broadcasted_iota(jnp.int32, (rows, 1), 0) % dec

    outs, lses = [], []
    for g, dil in enumerate(DIL):
        q = _stack_heads(q_ref[g], DIL_H).astype(BF16)
        k_new = _stack_heads(k_ref[g], DIL_H).astype(BF16)
        v_new = _stack_heads(v_ref[g], DIL_H).astype(BF16)
        n_res = min(dil, dec)
        s_past = jnp.full((rows, cols), NEG_INF, F32)
        for r in range(n_res):
            kc = kc_refs[g][:, r].reshape(cols, LANES).astype(BF16)
            s_r = lax.dot_general(q, kc, NT, preferred_element_type=F32) * scale
            ok = same_head & (t_row % dil == r) & (pos_col >= t_row // dil)
            s_past = jnp.where(ok, s_r, s_past)
        s_new = lax.dot_general(q, k_new, NT, preferred_element_type=F32) * scale
        s_new = jnp.where(new_head & (t_diff >= 0) & (t_diff % dil == 0), s_new, NEG_INF)
        m = jnp.maximum(jnp.max(s_past, axis=1, keepdims=True), jnp.max(s_new, axis=1, keepdims=True))
        p_past = jnp.exp(s_past - m)
        p_new = jnp.exp(s_new - m)
        l = jnp.sum(p_past, axis=1, keepdims=True) + jnp.sum(p_new, axis=1, keepdims=True)
        acc = jnp.dot(p_new.astype(BF16), v_new, preferred_element_type=F32)
        for r in range(n_res):
            vc = vc_refs[g][:, r].reshape(cols, LANES).astype(BF16)
            p_r = jnp.where(t_row % dil == r, p_past, 0.0).astype(BF16)
            acc = acc + jnp.dot(p_r, vc, preferred_element_type=F32)
        outs.append(acc / l)
        lses.append(m + jnp.log(l))
    m = functools.reduce(jnp.maximum, lses)
    w = [jnp.exp(x - m) for x in lses]
    den = functools.reduce(lambda a, b: a + b, w)
    o = functools.reduce(lambda a, b: a + b, [w[g] * outs[g] for g in range(n_g)]) / den
    o_ref[...] = jnp.concatenate([o[h * dec:(h + 1) * dec, :] for h in range(DIL_H)], axis=1).astype(o_ref.dtype)


def dil_sample_attention(q, k, v, caches_k, caches_v, layer, batch, dec):
    n_g = len(DIL)
    new_spec = pl.BlockSpec((n_g, dec, D), lambda b: (0, b, 0))
    cache_specs, views_k, views_v = [], [], []
    for g, dil in enumerate(DIL):
        window = caches_k[g].shape[2]
        shape = (caches_k[g].shape[0], batch, window // dil, dil, DIL_H, LANES)
        views_k.append(caches_k[g].reshape(shape))
        views_v.append(caches_v[g].reshape(shape))
        cache_specs.append(pl.BlockSpec((None, None, DIL_N, min(dil, dec), DIL_H, LANES),
                                        lambda b: (layer, b, 0, 0, 0, 0)))
    return pl.pallas_call(
        functools.partial(_dil_sample_kernel, dec),
        grid=(batch,),
        in_specs=[new_spec] * 3 + cache_specs * 2,
        out_specs=pl.BlockSpec((dec, D), lambda b: (b, 0)),
        out_shape=jax.ShapeDtypeStruct((batch * dec, D), F32),
        compiler_params=_cparams(1), name="dil_sample_attn",
    )(q, k, v, *views_k, *views_v)


def kernel(x_prompt, x_sample, cache_k_moba, cache_v_moba, page_table, state_hgrn, cache_k_dil0, cache_v_dil0,
           cache_k_dil1, cache_v_dil1, cache_k_dil2, cache_v_dil2, c_prompt, c_sample, norm_mix_g, norm_ffn_g,
           norm_final_g, ada_w, ada_b, moba_w_qkv, moba_w_o, hgrn_w_in, hgrn_w_o, hgrn_norm_g, hgrn_lower_bounds,
           dil_w_in, dil_w_o, ffn_w1, ffn_w2):
    batch, seq, _ = x_prompt.shape
    dbatch, dec, _ = x_sample.shape
    depth = ada_w.shape[0]
    past = page_table.shape[1] * PAGE
    pos_p = jnp.arange(seq, dtype=jnp.int32)
    pos_s = past + jnp.arange(dec, dtype=jnp.int32)

    stp = Stream(batch * seq, 512, seq, per_row=False)
    sts = Stream(dbatch * dec, 512, dec, per_row=True)
    stp_hg = Stream(batch * seq, 256, seq, per_row=False)
    sts_hg = Stream(dbatch * dec, 256, dec, per_row=True)

    mod_all = ada_modulation(jnp.concatenate([c_prompt, c_sample], axis=0), ada_w, ada_b)
    lb_all = jnp.cumsum(jax.nn.softmax(hgrn_lower_bounds.astype(F32), axis=0), axis=0)
    lb_all = lb_all - lb_all[0:1]

    n_pool = cache_k_moba.shape[1]
    k_cache_t = jnp.transpose(cache_k_moba, (0, 1, 3, 4, 2)).reshape(-1, n_pool, D, PAGE)
    v_cache_t = jnp.transpose(cache_v_moba, (0, 1, 3, 4, 2)).reshape(-1, n_pool, D, PAGE)
    dil_ck = (cache_k_dil0, cache_k_dil1, cache_k_dil2)
    dil_cv = (cache_v_dil0, cache_v_dil1, cache_v_dil2)

    tile_rows = lambda tab, st: jnp.tile(tab, (st.tm // tab.shape[0], 1))
    rope64_s = [tile_rows(t, sts) for t in _rope_row_tables(pos_s, MOBA_HD)]
    rope128_p = _rope_row_tables(pos_p, LANES)
    rope128_s = [tile_rows(t, sts) for t in _rope_row_tables(pos_s, LANES)]

    xp = x_prompt.reshape(batch * seq, D)
    xs = x_sample.reshape(dbatch * dec, D)
    moba_kp, moba_vp, moba_ks, moba_vs, hgrn_sp, hgrn_ss = [], [], [], [], [], []
    dil_kp, dil_vp, dil_ks, dil_vs = [], [], [], []

    for i in range(depth):
        kind, j = i % 3, i // 3
        mod_p = mod_all[i, :batch].reshape(batch, 1, -1)
        mod_s = jnp.repeat(mod_all[i, batch:], dec, axis=0)
        ng = norm_mix_g[i].reshape(1, D)
        hg_p = hg_s = None
        if kind == 0:
            w = moba_w_qkv[j].astype(BF16)
            qp, ktp, vtp = moba_prompt_proj(stp, xp, ng, mod_p, w[:, :D], w[:, D:].T, pos_p, batch, seq)
            op = moba_prompt_attention(qp.reshape(batch, seq, D), ktp, vtp).reshape(batch * seq, D)
            qs, ks, vs = rope_proj(sts, xs, ng, mod_s, w[None], rope64_s[0], rope64_s[1], MOBA_HD, F32)
            os_ = moba_sample_attention(qs[0], ks[0], vs[0], k_cache_t, v_cache_t, j, page_table, dec)
            heads_t = lambda a: a.reshape(batch, MOBA_H, MOBA_HD, seq).transpose(0, 3, 1, 2)
            moba_kp.append(heads_t(ktp))
            moba_vp.append(heads_t(vtp))
            moba_ks.append(ks[0].reshape(dbatch, dec, MOBA_H, MOBA_HD))
            moba_vs.append(vs[0].reshape(dbatch, dec, MOBA_H, MOBA_HD))
            wo = moba_w_o[j].astype(BF16)
        elif kind == 1:
            w = hgrn_w_in[j].astype(BF16)
            q_, lf_, k_, v_, gs_p = hgrn_proj(stp_hg, xp, ng, mod_p, w, lb_all[i])
            op, sp = hgrn_scan(q_, lf_, k_, v_, None, batch, seq)
            q_, lf_, k_, v_, gs_s = hgrn_proj(sts_hg, xs, ng, mod_s, w, lb_all[i])
            os_, ss = hgrn_scan(q_, lf_, k_, v_, state_hgrn[j], dbatch, dec)
            hgrn_sp.append(sp)
            hgrn_ss.append(ss)
            hn = hgrn_norm_g[j].reshape(1, D)
            hg_p, hg_s = (gs_p, hn), (gs_s, hn)
            wo = hgrn_w_o[j].astype(BF16)
        else:
            n_g = len(DIL)
            w = dil_w_in[j].astype(BF16).reshape(D, n_g, 3 * D).transpose(1, 0, 2)
            qp, kp, vp = rope_proj(stp, xp, ng, mod_p, w, rope128_p[0], rope128_p[1], LANES, F32)
            op = dil_prompt_attention(qp, kp, vp, batch, seq)
            qs, ks, vs = rope_proj(sts, xs, ng, mod_s, w, rope128_s[0], rope128_s[1], LANES, F32)
            os_ = dil_sample_attention(qs, ks, vs, dil_ck, dil_cv, j, dbatch, dec)
            for g in range(n_g):
                keep = min(DIL_N * DIL[g], seq)
                heads = lambda a, b: a.reshape(b, -1, DIL_H, LANES)
                dil_kp.append(heads(kp[g].reshape(batch, seq, D)[:, seq - keep:], batch))
                dil_vp.append(heads(vp[g].reshape(batch, seq, D)[:, seq - keep:], batch))
                window = dil_ck[g].shape[2]
                keep_s = min(DIL_N * DIL[g], window + dec)
                cat_k = jnp.concatenate([dil_ck[g][j], heads(ks[g], dbatch)], axis=1)
                cat_v = jnp.concatenate([dil_cv[g][j], heads(vs[g], dbatch)], axis=1)
                dil_ks.append(cat_k[:, window + dec - keep_s:])
                dil_vs.append(cat_v[:, window + dec - keep_s:])
            wo = dil_w_o[j].astype(BF16)
        nf = norm_ffn_g[i].reshape(1, D)
        w1, w2 = ffn_w1[i].astype(BF16), ffn_w2[i].astype(BF16)
        fin = norm_final_g.reshape(1, D) if i == depth - 1 else None
        xp = post_mixer(stp, op, xp, mod_p, wo, nf, w1, w2, hgrn_extra=hg_p, final_g=fin)
        xs = post_mixer(sts, os_, xs, mod_s, wo, nf, w1, w2, hgrn_extra=hg_s, final_g=fin)

    n_g = len(DIL)
    per_group = lambda lst: [jnp.stack(lst[g::n_g]) for g in range(n_g)]
    dkp, dvp, dks, dvs = per_group(dil_kp), per_group(dil_vp), per_group(dil_ks), per_group(dil_vs)
    return (xp.reshape(batch, seq, D), xs.reshape(dbatch, dec, D),
            jnp.stack(moba_kp), jnp.stack(moba_vp), jnp.stack(moba_ks), jnp.stack(moba_vs),
            jnp.stack(hgrn_sp), jnp.stack(hgrn_ss),
            dkp[0], dvp[0], dkp[1], dvp[1], dkp[2], dvp[2],
            dks[0], dvs[0], dks[1], dvs[1], dks[2], dvs[2])
```

```python
import functools

import jax
import jax.numpy as jnp
from jax import lax
from jax.experimental import pallas as pl
from jax.experimental.pallas import tpu as pltpu

F32 = jnp.float32
BF16 = jnp.bfloat16
NEG_INF = float("-inf")

D = 1024
LANES = 128
SUBLANES = 8
BF16_ROWS = 16
MXU_DEPTH = 256
NORM_EPS = 1e-6
ROPE_THETA = 10000.0
PAGE = 128
MOBA_HD = 64
MOBA_H = D // MOBA_HD
MOBA_BLOCK = 256
MOBA_TOPK = 3
MOBA_GROUP = 4
MASK_OFF = -1e30
LOG2_E = 1.4426950408889634
HG_H = 8
HG_HEADS_PER_STEP = 4
DIL = (1, 4, 16)
DIL_N = 128
DIL_H = 8
DIL_SUPER = DIL_N * DIL[-1]
DIL_UNROLL = 4
VMEM_LIMIT = 56 * 1024 * 1024

NT = (((1,), (1,)), ((), ()))
HIGHEST = lax.Precision.HIGHEST


def _cparams(n_axes):
    return pltpu.CompilerParams(dimension_semantics=("arbitrary",) * n_axes, vmem_limit_bytes=VMEM_LIMIT)


def _resident(shape):
    zeros = (0,) * len(shape)
    return pl.BlockSpec(shape, lambda *_: zeros, pipeline_mode=pl.Buffered(1))


def _silu(x):
    return x * jax.nn.sigmoid(x)


def _rms(x, g):
    return x * lax.rsqrt(jnp.mean(x * x, axis=-1, keepdims=True) + NORM_EPS) * g


def _norm_mod(x, g, sc, sh):
    return _rms(x, g) * (1.0 + sc) + sh


def _ada_kernel(c_ref, w_ref, b_ref, o_ref):
    s = _silu(c_ref[...]).astype(BF16)
    o_ref[...] = jnp.dot(s, w_ref[...].astype(BF16), preferred_element_type=F32) + b_ref[...]


def ada_modulation(c_all, ada_w, ada_b):
    n_layers, _, n = ada_w.shape
    m = c_all.shape[0]
    tn = 1536
    return pl.pallas_call(
        _ada_kernel,
        grid=(n_layers, n // tn),
        in_specs=[pl.BlockSpec((m, D), lambda l, j: (0, 0)),
                  pl.BlockSpec((None, D, tn), lambda l, j: (l, 0, j)),
                  pl.BlockSpec((None, 1, tn), lambda l, j: (l, 0, j))],
        out_specs=pl.BlockSpec((None, m, tn), lambda l, j: (l, 0, j)),
        out_shape=jax.ShapeDtypeStruct((n_layers, m, n), F32),
        compiler_params=_cparams(2), name="ada_mod",
    )(c_all, ada_w, ada_b.reshape(n_layers, 1, n))


class Stream:
    def __init__(self, rows, tm, rows_per_batch, per_row):
        self.rows, self.tm, self.rows_per_batch, self.per_row = rows, tm, rows_per_batch, per_row
        self.n_tiles = rows // tm
        self.tiles_per_batch = max(rows_per_batch // tm, 1)

    def row_spec(self, width=D):
        return pl.BlockSpec((self.tm, width), lambda i: (i, 0))

    def mod_spec(self, chunk):
        if self.per_row:
            return pl.BlockSpec((self.tm, D), lambda i: (i, chunk))
        tpb = self.tiles_per_batch
        return pl.BlockSpec((None, 1, D), lambda i: (i // tpb, 0, chunk))


def _vec_spec():
    return pl.BlockSpec((1, D), lambda i: (0, 0))


def _rope_tables(pos, hd):
    inv = ROPE_THETA ** (-jnp.arange(0, hd, 2, dtype=F32) / hd)
    ang = pos.astype(F32)[:, None] * inv[None, :]
    return jnp.cos(ang), jnp.sin(ang)


def _rope_row_tables(pos, hd):
    cos, sin = _rope_tables(pos, hd)
    rep = LANES // hd
    return (jnp.tile(jnp.concatenate([cos, cos], axis=1), (1, rep)),
            jnp.tile(jnp.concatenate([-sin, sin], axis=1), (1, rep)))


def _rope_rows(x, cos, sin, hd):
    lane = lax.broadcasted_iota(jnp.int32, (1, LANES), 1)
    first_half = (lane % hd) < (hd // 2)
    outs = []
    for c in range(x.shape[1] // LANES):
        xc = x[:, c * LANES:(c + 1) * LANES]
        if hd == LANES:
            rot = pltpu.roll(xc, hd // 2, 1)
        else:
            rot = jnp.where(first_half, pltpu.roll(xc, LANES - hd // 2, 1), pltpu.roll(xc, hd // 2, 1))
        outs.append(xc * cos + rot * sin)
    return jnp.concatenate(outs, axis=1)


def _rope_proj_kernel(hd, x_ref, g_ref, sc_ref, sh_ref, w_ref, cos_ref, sin_ref, q_ref, k_ref, v_ref):
    h = _norm_mod(x_ref[...], g_ref[...], sc_ref[...], sh_ref[...]).astype(BF16)
    qkv = jnp.dot(h, w_ref[...], preferred_element_type=F32)
    cos, sin = cos_ref[...], sin_ref[...]
    q_ref[...] = _rope_rows(qkv[:, :D], cos, sin, hd).astype(q_ref.dtype)
    k_ref[...] = _rope_rows(qkv[:, D:2 * D], cos, sin, hd)
    v_ref[...] = qkv[:, 2 * D:]


def rope_proj(st, x, norm_g, mod, w, cos, sin, hd, q_dtype):
    n_groups = w.shape[0]
    n_tab = cos.shape[0] // st.tm
    row = lambda g, i: (i, 0)
    mod_s = lambda chunk: (pl.BlockSpec((st.tm, D), lambda g, i: (i, chunk)) if st.per_row else
                           pl.BlockSpec((None, 1, D), lambda g, i: (i // st.tiles_per_batch, 0, chunk)))
    tab = pl.BlockSpec((st.tm, LANES), lambda g, i: (i % n_tab, 0))
    out = pl.BlockSpec((None, st.tm, D), lambda g, i: (g, i, 0))
    return pl.pallas_call(
        functools.partial(_rope_proj_kernel, hd),
        grid=(n_groups, st.n_tiles),
        in_specs=[pl.BlockSpec((st.tm, D), row), pl.BlockSpec((1, D), lambda g, i: (0, 0)), mod_s(1), mod_s(0),
                  pl.BlockSpec((None, D, 3 * D), lambda g, i: (g, 0, 0)), tab, tab],
        out_specs=[out, out, out],
        out_shape=[jax.ShapeDtypeStruct((n_groups, st.rows, D), q_dtype),
                   jax.ShapeDtypeStruct((n_groups, st.rows, D), F32),
                   jax.ShapeDtypeStruct((n_groups, st.rows, D), F32)],
        compiler_params=_cparams(2), name="rope_proj",
    )(x, norm_g, mod, mod, w, cos, sin)


def _moba_prompt_proj_kernel(x_ref, g_ref, sc_ref, sh_ref, wq_ref, wkvt_ref, cr_ref, sr_ref, ct_ref, st_ref,
                             q_ref, kt_ref, vt_ref):
    h = _norm_mod(x_ref[...], g_ref[...], sc_ref[...], sh_ref[...]).astype(BF16)
    q = jnp.dot(h, wq_ref[...], preferred_element_type=F32)
    q_ref[...] = _rope_rows(q, cr_ref[...], sr_ref[...], MOBA_HD).astype(BF16)
    kvt = lax.dot_general(wkvt_ref[...], h, NT, preferred_element_type=F32)
    ct, st = ct_ref[...], st_ref[...]
    half = MOBA_HD // 2
    for hh in range(MOBA_H):
        lo = hh * MOBA_HD
        x1, x2 = kvt[lo:lo + half], kvt[lo + half:lo + MOBA_HD]
        kt_ref[lo:lo + half, :] = x1 * ct - x2 * st
        kt_ref[lo + half:lo + MOBA_HD, :] = x2 * ct + x1 * st
    vt_ref[...] = kvt[D:]


def moba_prompt_proj(st, x, norm_g, mod, wq, wkvt, pos, batch, seq):
    cos_r, sin_r = _rope_row_tables(pos, MOBA_HD)
    cos, sin = _rope_tables(pos, MOBA_HD)
    tpb = st.tiles_per_batch
    tab_r = pl.BlockSpec((st.tm, LANES), lambda i: (i % tpb, 0))
    tab_t = pl.BlockSpec((MOBA_HD // 2, st.tm), lambda i: (0, i % tpb))
    out_t = pl.BlockSpec((None, D, st.tm), lambda i: (i // tpb, 0, i % tpb))
    return pl.pallas_call(
        _moba_prompt_proj_kernel,
        grid=(st.n_tiles,),
        in_specs=[st.row_spec(), _vec_spec(), st.mod_spec(1), st.mod_spec(0), _resident((D, D)),
                  _resident((2 * D, D)), tab_r, tab_r, tab_t, tab_t],
        out_specs=[st.row_spec(), out_t, out_t],
        out_shape=[jax.ShapeDtypeStruct((st.rows, D), BF16),
                   jax.ShapeDtypeStruct((batch, D, seq), F32),
                   jax.ShapeDtypeStruct((batch, D, seq), F32)],
        compiler_params=_cparams(1), name="moba_prompt_proj",
    )(x, norm_g, mod, mod, wq, wkvt, cos_r, sin_r, cos.T, sin.T)


def _hgrn_proj_kernel(x_ref, g_ref, sc_ref, sh_ref, w_ref, loglb_ref, log1mlb_ref, omlb_ref,
                      q_ref, lf_ref, k_ref, v_ref, gs_ref):
    h = _norm_mod(x_ref[...], g_ref[...], sc_ref[...], sh_ref[...]).astype(BF16)
    z = jnp.dot(h, w_ref[...], preferred_element_type=F32)
    q_ref[...] = _silu(z[:, :D])
    f = z[:, D:2 * D]
    log_sig = jnp.minimum(f, 0.0) - jnp.log1p(jnp.exp(-jnp.abs(f)))
    a = loglb_ref[...]
    b = log1mlb_ref[...] + log_sig
    lf_ref[...] = jnp.maximum(a, b) + jnp.log1p(jnp.exp(-jnp.abs(a - b)))
    k_ref[...] = omlb_ref[...] * jax.nn.sigmoid(-f)
    v_ref[...] = z[:, 2 * D:3 * D]
    gs_ref[...] = _silu(z[:, 3 * D:])


def hgrn_proj(st, x, norm_g, mod, w, lb):
    lb = lb.reshape(1, D)
    outs = [jax.ShapeDtypeStruct((st.rows, D), F32)] * 5
    return pl.pallas_call(
        _hgrn_proj_kernel,
        grid=(st.n_tiles,),
        in_specs=[st.row_spec(), _vec_spec(), st.mod_spec(1), st.mod_spec(0), _resident((D, 4 * D)),
                  _vec_spec(), _vec_spec(), _vec_spec()],
        out_specs=[st.row_spec()] * 5,
        out_shape=outs,
        compiler_params=_cparams(1), name="hgrn_proj",
    )(x, norm_g, mod, mod, w, jnp.log(lb), jnp.log1p(-lb), 1.0 - lb)


def _post_kernel(hgrn, final, *refs):
    refs = list(refs)
    o_ref = refs.pop(0)
    if hgrn:
        gs_ref, ng_ref = refs.pop(0), refs.pop(0)
    (x_ref, g1_ref, wo_ref, nf_ref, sc_ref, sh_ref, g2_ref, w1_ref, w2_ref) = refs[:9]
    refs = refs[9:]
    if final:
        fin_ref = refs.pop(0)
    out_ref = refs.pop(0)

    o = o_ref[...]
    if hgrn:
        o = _rms(o, ng_ref[...]) * gs_ref[...]
    x = x_ref[...] + g1_ref[...] * jnp.dot(o.astype(BF16), wo_ref[...], preferred_element_type=F32)
    h = _norm_mod(x, nf_ref[...], sc_ref[...], sh_ref[...]).astype(BF16)
    acc = x
    d_ff = w1_ref.shape[1]
    chunk = D
    y = None
    for c in range(d_ff // chunk):
        u = jnp.dot(h, w1_ref[:, c * chunk:(c + 1) * chunk], preferred_element_type=F32)
        u = jnp.maximum(u, 0.0)
        u = (u * u).astype(BF16)
        part = jnp.dot(u, w2_ref[c * chunk:(c + 1) * chunk, :], preferred_element_type=F32)
        y = part if y is None else y + part
    acc = acc + g2_ref[...] * y
    if final:
        acc = _rms(acc, fin_ref[...])
    out_ref[...] = acc


def post_mixer(st, o, x, mod, wo, norm_ffn_g, w1, w2, hgrn_extra=None, final_g=None):
    hgrn, final = hgrn_extra is not None, final_g is not None
    args, specs = [o], [st.row_spec()]
    if hgrn:
        args += list(hgrn_extra)
        specs += [st.row_spec(), _vec_spec()]
    args += [x, mod, wo, norm_ffn_g, mod, mod, mod, w1, w2]
    specs += [st.row_spec(), st.mod_spec(2), _resident(wo.shape), _vec_spec(), st.mod_spec(4), st.mod_spec(3),
              st.mod_spec(5), _resident(w1.shape), _resident(w2.shape)]
    if final:
        args.append(final_g)
        specs.append(_vec_spec())
    return pl.pallas_call(
        functools.partial(_post_kernel, hgrn, final),
        grid=(st.n_tiles,),
        in_specs=specs,
        out_specs=st.row_spec(),
        out_shape=jax.ShapeDtypeStruct((st.rows, D), F32),
        compiler_params=_cparams(1), name="post_mixer",
    )(*args)


def _top_blocks(gate, idx, axis):
    sel = jnp.zeros(gate.shape, jnp.bool_)
    for _ in range(MOBA_TOPK):
        m = jnp.max(gate, axis=axis, keepdims=True)
        first = jnp.min(jnp.where(gate == m, idx, float(2 ** 20)), axis=axis, keepdims=True)
        hit = idx == first
        sel = sel | (hit & (m > NEG_INF))
        gate = jnp.where(hit, NEG_INF, gate)
    return sel


def _moba_prompt_kernel(n_blk, q_ref, kt_ref, vt_ref, o_ref, kb_ref, vb_ref, vg_ref, km_ref):
    i = pl.program_id(2)
    blk, grp = MOBA_BLOCK, MOBA_GROUP
    heads = LANES // MOBA_HD
    lane = lax.broadcasted_iota(jnp.int32, (1, LANES), 1)
    own_lanes = [(lane >= a * MOBA_HD) & (lane < (a + 1) * MOBA_HD) for a in range(heads)]
    tag_base = [(heads - 1 - a) * MOBA_HD for a in range(heads)]

    @pl.when(i == 0)
    def _():
        for n in range(n_blk):
            kn = kt_ref[:, n * blk:(n + 1) * blk].T
            km_ref[pl.ds(n, 1), :] = jnp.mean(kn, axis=0, keepdims=True)
            g_, r_ = divmod(n, grp)
            for a in range(heads):
                tag = jnp.where(lane == tag_base[a] + n, 1.0, 0.0)
                kb_ref[a, g_, r_ * blk:(r_ + 1) * blk, :] = jnp.where(own_lanes[a], kn, tag).astype(BF16)
        extra = lax.broadcasted_iota(jnp.int32, (BF16_ROWS, grp * blk), 0)
        ones_row = jnp.where(extra == 0, 1.0, 0.0).astype(BF16)
        for g_ in range(n_blk // grp):
            for a in range(heads):
                vals = vt_ref[a * MOBA_HD:(a + 1) * MOBA_HD, g_ * grp * blk:(g_ + 1) * grp * blk].astype(BF16)
                vals = jnp.concatenate([vals, ones_row], axis=0)
                vg_ref[a, g_] = vals
                for r_ in range(grp):
                    vb_ref[a, g_ * grp + r_] = vals[:, r_ * blk:(r_ + 1) * blk]

    q2 = q_ref[...].astype(F32)
    blk_id = lax.broadcasted_iota(jnp.int32, (n_blk, blk), 0)
    key_i = lax.broadcasted_iota(jnp.int32, (blk, blk), 0)
    qry_i = lax.broadcasted_iota(jnp.int32, (blk, blk), 1)
    log2_scale = MOBA_HD ** -0.5 * LOG2_E
    gi = i // grp
    own_rows = pl.ds(pl.multiple_of((i % grp) * blk, blk), blk)

    q_own, q_aug = [], []
    for a in range(heads):
        qa = jnp.where(own_lanes[a], q2, 0.0)
        gate = lax.dot_general(km_ref[...], qa, NT, precision=HIGHEST, preferred_element_type=F32)
        gate = jnp.where(blk_id < i, gate, NEG_INF)
        sel = _top_blocks(gate, blk_id.astype(F32), 0)
        bias = jnp.where(sel, 0.0, MASK_OFF)
        bias_t = jnp.concatenate([bias, jnp.zeros((LANES - n_blk, blk), F32)], axis=0).T
        if tag_base[a]:
            bias_t = pltpu.roll(bias_t, tag_base[a], 1)
        q_own.append((qa * log2_scale).astype(BF16))
        q_aug.append(jnp.where(own_lanes[a], q2 * log2_scale, bias_t).astype(BF16))

    def own_scores(a):
        s = lax.dot_general(kb_ref[a, gi, own_rows, :], q_own[a], NT, preferred_element_type=F32)
        return jnp.where(key_i <= qry_i, s, MASK_OFF)

    def group_scores(a, g):
        return lax.dot_general(kb_ref[a, g], q_aug[a], NT, preferred_element_type=F32)

    def col_max(s):
        return jnp.max(s, axis=0, keepdims=True)

    def weights(s, m):
        return jnp.exp2((s - m).astype(BF16))

    carry = []
    for a in range(heads):
        s = own_scores(a)
        m = col_max(s)
        acc = jnp.dot(vb_ref[a, i], weights(s, m), preferred_element_type=F32)
        carry.append((m, acc, group_scores(a, 0)))
    last_grp = n_blk // grp - 1

    def body(g, carry):
        new = []
        for a in range(heads):
            m, acc, s = carry[a]
            s_next = group_scores(a, jnp.minimum(g + 1, last_grp))
            m_new = jnp.maximum(m, col_max(s))
            acc = jnp.exp2(m - m_new) * acc + jnp.dot(vg_ref[a, g], weights(s, m_new), preferred_element_type=F32)
            new.append((m_new, acc, s_next))
        return tuple(new)

    carry = lax.fori_loop(0, (i + grp - 1) // grp, body, tuple(carry))
    o_t = jnp.concatenate([acc[:MOBA_HD] / acc[MOBA_HD:MOBA_HD + 1] for _, acc, _ in carry], axis=0)
    o_ref[...] = o_t.T.astype(o_ref.dtype)


def moba_prompt_attention(q, kt, vt):
    batch, seq, _ = q.shape
    n_blk = seq // MOBA_BLOCK
    n_grp = n_blk // MOBA_GROUP
    heads = LANES // MOBA_HD
    kv_spec = pl.BlockSpec((None, LANES, seq), lambda b, hp, i: (b, hp, 0))
    q_spec = pl.BlockSpec((None, MOBA_BLOCK, LANES), lambda b, hp, i: (b, i, hp))
    return pl.pallas_call(
        functools.partial(_moba_prompt_kernel, n_blk),
        grid=(batch, D // LANES, n_blk),
        in_specs=[q_spec, kv_spec, kv_spec],
        out_specs=q_spec,
        out_shape=jax.ShapeDtypeStruct((batch, seq, D), BF16),
        scratch_shapes=[pltpu.VMEM((heads, n_grp, MOBA_GROUP * MOBA_BLOCK, LANES), BF16),
                        pltpu.VMEM((heads, n_blk, MOBA_HD + BF16_ROWS, MOBA_BLOCK), BF16),
                        pltpu.VMEM((heads, n_grp, MOBA_HD + BF16_ROWS, MOBA_GROUP * MOBA_BLOCK), BF16),
                        pltpu.VMEM((n_blk, LANES), F32)],
        compiler_params=_cparams(3), name="moba_prompt_attn",
    )(q, kt, vt)


def _moba_sample_kernel(n_pages, dec, pt_ref, q_ref, k_ref, v_ref, *refs):
    del pt_ref
    k_pages, v_pages, o_ref = refs[:n_pages], refs[n_pages:2 * n_pages], refs[2 * n_pages]
    rows = MOBA_H * dec
    new_pad = -(-dec // BF16_ROWS) * BF16_ROWS
    pages_per_blk = MOBA_BLOCK // PAGE
    n_past = n_pages // pages_per_blk
    scale = MOBA_HD ** -0.5

    set_w = MXU_DEPTH
    set_heads = set_w // MOBA_HD
    set_rows = set_heads * dec
    n_sets = D // set_w
    row_h = lax.broadcasted_iota(jnp.int32, (set_rows, set_w), 0) // dec
    col_h = lax.broadcasted_iota(jnp.int32, (set_rows, set_w), 1) // MOBA_HD
    q_sets = []
    for c in range(n_sets):
        qc = q_ref[:, c * set_w:(c + 1) * set_w]
        q_rep = jnp.broadcast_to(qc[None], (set_heads, dec, set_w)).reshape(set_rows, set_w)
        q_sets.append(jnp.where(row_h == col_h, q_rep, 0.0).astype(BF16))

    def stack_sets(fn):
        return jnp.concatenate([fn(c) for c in range(n_sets)], axis=0)

    s_pages = [stack_sets(lambda c: jnp.dot(q_sets[c], kp[c * set_w:(c + 1) * set_w, :].astype(BF16),
                                            preferred_element_type=F32)) for kp in k_pages]
    lane = lax.broadcasted_iota(jnp.int32, (rows, LANES), 1)
    gate = jnp.full((rows, LANES), NEG_INF, F32)
    for n in range(n_past):
        tot = s_pages[n * pages_per_blk]
        for pg in range(1, pages_per_blk):
            tot = tot + s_pages[n * pages_per_blk + pg]
        gate = jnp.where(lane == n, jnp.sum(tot, axis=1, keepdims=True) * (1.0 / MOBA_BLOCK), gate)
    sel = _top_blocks(gate, lane.astype(F32), 1)
    bias = jnp.where(sel, 0.0, NEG_INF)

    s_past = [s_pages[pg] * scale + bias[:, pg // pages_per_blk:pg // pages_per_blk + 1] for pg in range(n_pages)]
    pad = jnp.zeros((new_pad - dec, D), F32)
    k_new = jnp.concatenate([k_ref[...], pad], axis=0).astype(BF16)
    v_new = jnp.concatenate([v_ref[...], pad], axis=0).astype(BF16)
    s_new = stack_sets(lambda c: lax.dot_general(q_sets[c], k_new[:, c * set_w:(c + 1) * set_w], NT,
                                                 preferred_element_type=F32)) * scale
    t_q = lax.broadcasted_iota(jnp.int32, (rows, new_pad), 0) % dec
    t_k = lax.broadcasted_iota(jnp.int32, (rows, new_pad), 1)
    s_new = jnp.where(t_k <= t_q, s_new, NEG_INF)

    m = jnp.max(s_new, axis=1, keepdims=True)
    for s in s_past:
        m = jnp.maximum(m, jnp.max(s, axis=1, keepdims=True))
    p_new = jnp.exp(s_new - m)
    l = jnp.sum(p_new, axis=1, keepdims=True)
    p_past = []
    for s in s_past:
        p = jnp.exp(s - m)
        l = l + jnp.sum(p, axis=1, keepdims=True)
        p_past.append(p.astype(BF16))
    p_new = p_new.astype(BF16)
    inv_l = 1.0 / l

    lane = lax.broadcasted_iota(jnp.int32, (dec, LANES), 1)
    heads_per_vreg = LANES // MOBA_HD
    outs = []
    for c in range(n_sets):
        r = slice(c * set_rows, (c + 1) * set_rows)
        acc = jnp.dot(p_new[r], v_new[:, c * set_w:(c + 1) * set_w], preferred_element_type=F32)
        for p, vp in zip(p_past, v_pages):
            acc = acc + lax.dot_general(p[r], vp[c * set_w:(c + 1) * set_w, :].astype(BF16), NT,
                                        preferred_element_type=F32)
        acc = acc * inv_l[r]
        for e in range(set_w // LANES):
            piece = acc[e * heads_per_vreg * dec:(e * heads_per_vreg + 1) * dec, e * LANES:(e + 1) * LANES]
            for a in range(1, heads_per_vreg):
                h = e * heads_per_vreg + a
                piece = jnp.where(lane >= a * MOBA_HD, acc[h * dec:(h + 1) * dec, e * LANES:(e + 1) * LANES], piece)
            outs.append(piece)
    o_ref[...] = jnp.concatenate(outs, axis=1).astype(o_ref.dtype)


def moba_sample_attention(q, k, v, k_cache_t, v_cache_t, layer, page_table, dec):
    batch, n_pages = page_table.shape
    new_spec = pl.BlockSpec((dec, D), lambda b, pt: (b, 0))

    def page_spec(pg):
        return pl.BlockSpec((None, None, D, PAGE), lambda b, pt: (layer, pt[b, pg], 0, 0))

    grid_spec = pltpu.PrefetchScalarGridSpec(
        num_scalar_prefetch=1, grid=(batch,),
        in_specs=[new_spec, new_spec, new_spec] + [page_spec(pg) for pg in range(n_pages)] * 2,
        out_specs=new_spec)
    return pl.pallas_call(
        functools.partial(_moba_sample_kernel, n_pages, dec),
        grid_spec=grid_spec,
        out_shape=jax.ShapeDtypeStruct((batch * dec, D), F32),
        compiler_params=_cparams(1), name="moba_sample_attn",
    )(page_table, q, k, v, *([k_cache_t] * n_pages), *([v_cache_t] * n_pages))


def _cumsum_rows(x):
    row = lax.broadcasted_iota(jnp.int32, x.shape, 0)
    shift = 1
    while shift < x.shape[0]:
        x = x + jnp.where(row >= shift, pltpu.roll(x, shift, 0), 0.0)
        shift *= 2
    return x


def _hgrn_kernel(carry, n_sub, *refs):
    if carry:
        q_ref, lf_ref, k_ref, v_ref, o_ref, s_out_ref, st_ref = refs
    else:
        q_ref, lf_ref, k_ref, v_ref, s_in_ref, o_ref, s_out_ref = refs
    c = pl.program_id(2)
    sub = SUBLANES
    n_heads = q_ref.shape[1] // LANES
    head_cols = [slice(h * LANES, (h + 1) * LANES) for h in range(n_heads)]
    vt = [v_ref[:, hc].T for hc in head_cols]
    row = lax.broadcasted_iota(jnp.int32, (sub, LANES), 0)

    if carry:
        @pl.when(c == 0)
        def _():
            st_ref[...] = jnp.zeros_like(st_ref)

    for s in range(n_sub):
        sl = slice(s * sub, (s + 1) * sub)
        for h, hc in enumerate(head_cols):
            q, k, v = q_ref[sl, hc], k_ref[sl, hc], v_ref[sl, hc]
            g = _cumsum_rows(lf_ref[sl, hc])
            state_t = st_ref[h] if carry else s_in_ref[s, h].T
            o = lax.dot_general(q * jnp.exp(g), state_t, NT, preferred_element_type=F32)
            for d in range(sub):
                if d == 0:
                    x = q * k
                else:
                    x = q * pltpu.roll(k, d, 0) * jnp.exp(jnp.where(row >= d, g - pltpu.roll(g, d, 0), 0.0))
                    x = jnp.where(row >= d, x, 0.0)
                a = jnp.sum(x, axis=1, keepdims=True)
                o = o + a * (v if d == 0 else pltpu.roll(v, d, 0))
            o_ref[sl, hc] = o
            g_end = g[sub - 1:sub, :]
            k_dec = k * jnp.exp(g_end - g)
            state_t = state_t * jnp.exp(g_end) + jnp.dot(vt[h][:, sl], k_dec, preferred_element_type=F32)
            if carry:
                st_ref[h] = state_t
            else:
                s_out_ref[s, h] = state_t.T

    if carry:
        @pl.when(c == pl.num_programs(2) - 1)
        def _():
            for h in range(n_heads):
                s_out_ref[h] = st_ref[h].T


def hgrn_scan(q, lf, k, v, state0, batch, seq):
    rows = batch * seq
    tc = LANES
    hps = HG_HEADS_PER_STEP
    width = hps * LANES
    state_shape = jax.ShapeDtypeStruct((batch, HG_H, LANES, LANES), F32)
    if state0 is None:
        n_chunks = seq // tc
        tile = pl.BlockSpec((tc, width), lambda b, h, c: (b * n_chunks + c, h))
        return pl.pallas_call(
            functools.partial(_hgrn_kernel, True, tc // SUBLANES),
            grid=(batch, HG_H // hps, n_chunks),
            in_specs=[tile] * 4,
            out_specs=[tile, pl.BlockSpec((None, hps, LANES, LANES), lambda b, h, c: (b, h, 0, 0))],
            out_shape=[jax.ShapeDtypeStruct((rows, D), F32), state_shape],
            scratch_shapes=[pltpu.VMEM((hps, LANES, LANES), F32)],
            compiler_params=_cparams(3), name="hgrn_scan_prompt",
        )(q, lf, k, v)
    n_sub = tc // seq
    tile = pl.BlockSpec((tc, width), lambda nb, h, c: (nb, h))
    st_spec = pl.BlockSpec((n_sub, hps, LANES, LANES), lambda nb, h, c: (nb, h, 0, 0))
    return pl.pallas_call(
        functools.partial(_hgrn_kernel, False, n_sub),
        grid=(batch // n_sub, HG_H // hps, 1),
        in_specs=[tile] * 4 + [st_spec],
        out_specs=[tile, st_spec],
        out_shape=[jax.ShapeDtypeStruct((rows, D), F32), state_shape],
        compiler_params=_cparams(3), name="hgrn_scan_sample",
    )(q, lf, k, v, state0)


def _dil_prompt_kernel(*refs):
    n_g = len(DIL)
    q_refs, k_refs, v_refs = refs[0:n_g], refs[n_g:2 * n_g], refs[2 * n_g:3 * n_g]
    kp_refs, vp_refs = refs[3 * n_g:4 * n_g], refs[4 * n_g:5 * n_g]
    o_ref = refs[5 * n_g]
    kf_refs, vf_refs = refs[5 * n_g + 1:6 * n_g + 1], refs[6 * n_g + 1:7 * n_g + 1]
    og_ref, lg_ref = refs[7 * n_g + 1], refs[7 * n_g + 2]
    j = pl.program_id(1)
    n = DIL_N
    scale = float(LANES) ** -0.5
    qi = lax.broadcasted_iota(jnp.int32, (n, 2 * n), 0)
    kj = lax.broadcasted_iota(jnp.int32, (n, 2 * n), 1)
    rel = qi + n - kj
    band = (rel >= 0) & (rel <= n)

    for g, dil in enumerate(DIL):
        span = n * dil
        kf, vf = kf_refs[g], vf_refs[g]
        kf[0:span, :] = kp_refs[g][...]
        kf[span:span + DIL_SUPER, :] = k_refs[g][...]
        vf[0:span, :] = vp_refs[g][...]
        vf[span:span + DIL_SUPER, :] = v_refs[g][...]

        def unit(u, g=g, dil=dil, span=span, kf=kf, vf=vf):
            start = (u // dil) * span + u % dil
            rows = lambda cnt: pl.ds(start, cnt) if dil == 1 else pl.ds(start, cnt, stride=dil)
            q = q_refs[g][rows(n), :].astype(BF16)
            kk = kf[rows(2 * n), :].astype(BF16)
            vv = vf[rows(2 * n), :].astype(BF16)
            s = lax.dot_general(q, kk, NT, preferred_element_type=F32) * scale
            first_key = jnp.where((j > 0) | (u >= dil), 0, n)
            s = jnp.where(band & (kj >= first_key), s, NEG_INF)
            m = jnp.max(s, axis=1, keepdims=True)
            p = jnp.exp(s - m)
            l = jnp.sum(p, axis=1, keepdims=True)
            o = jnp.dot(p.astype(BF16), vv, preferred_element_type=F32) / l
            og_ref[g, rows(n), :] = o
            lg_ref[g, rows(n), :] = jnp.broadcast_to(m + jnp.log(l), (n, LANES))

        def units(t, carry, unit=unit):
            for e in range(DIL_UNROLL):
                unit(t * DIL_UNROLL + e)
            return carry

        lax.fori_loop(0, DIL_SUPER // n // DIL_UNROLL, units, 0)

    lse = [lg_ref[g] for g in range(n_g)]
    m = functools.reduce(jnp.maximum, lse)
    w = [jnp.exp(x - m) for x in lse]
    den = functools.reduce(lambda a, b: a + b, w)
    num = functools.reduce(lambda a, b: a + b, [w[g] * og_ref[g] for g in range(n_g)])
    o_ref[...] = (num / den).astype(o_ref.dtype)


def dil_prompt_attention(q, k, v, batch, seq):
    n_g = len(DIL)
    n_super = seq // DIL_SUPER
    cur, prev, scratch = [], [], []
    for g, dil in enumerate(DIL):
        span = DIL_N * dil
        per = DIL_SUPER // span
        cur.append(pl.BlockSpec((None, DIL_SUPER, LANES), lambda b, j, h, g=g: (g, b * n_super + j, h)))
        prev.append(pl.BlockSpec(
            (None, span, LANES),
            lambda b, j, h, g=g, per=per: (g, jnp.maximum((b * n_super + j) * per - 1, 0), h)))
        scratch.append(pltpu.VMEM((span + DIL_SUPER, LANES), F32))
    out_spec = pl.BlockSpec((DIL_SUPER, LANES), lambda b, j, h: (b * n_super + j, h))
    return pl.pallas_call(
        _dil_prompt_kernel,
        grid=(batch, n_super, DIL_H),
        in_specs=cur * 3 + prev * 2,
        out_specs=out_spec,
        out_shape=jax.ShapeDtypeStruct((batch * seq, D), BF16),
        scratch_shapes=scratch * 2 + [pltpu.VMEM((n_g, DIL_SUPER, LANES), F32)] * 2,
        compiler_params=_cparams(3), name="dil_prompt_attn",
    )(q, q, q, k, k, k, v, v, v, k, k, k, v, v, v)


def _stack_heads(x, n_heads):
    return jnp.concatenate([x[:, h * LANES:(h + 1) * LANES] for h in range(n_heads)], axis=0)


def _dil_sample_kernel(dec, q_ref, k_ref, v_ref, *refs):
    n_g = len(DIL)
    kc_refs, vc_refs, o_ref = refs[:n_g], refs[n_g:2 * n_g], refs[2 * n_g]
    rows = DIL_H * dec
    cols = DIL_N * DIL_H
    scale = float(LANES) ** -0.5
    r_i = lax.broadcasted_iota(jnp.int32, (rows, cols), 0)
    c_i = lax.broadcasted_iota(jnp.int32, (rows, cols), 1)
    same_head = (c_i % DIL_H) == (r_i // dec)
    t_row = r_i % dec
    pos_col = c_i // DIL_H
    rn = lax.broadcasted_iota(jnp.int32, (rows, rows), 0)
    cn = lax.broadcasted_iota(jnp.int32, (rows, rows), 1)
    new_head = (rn // dec) == (cn // dec)
    t_diff = rn % dec - cn % dec
    t_col1 = lax.broadcasted_iota(jnp.int32, (rows, 1), 0) % dec

    outs, lses = [], []
    for g, dil in enumerate(DIL):
        q = _stack_heads(q_ref[g], DIL_H).astype(BF16)
        k_new = _stack_heads(k_ref[g], DIL_H).astype(BF16)
        v_new = _stack_heads(v_ref[g], DIL_H).astype(BF16)
        n_res = min(dil, dec)
        s_past = jnp.full((rows, cols), NEG_INF, F32)
        for r in range(n_res):
            kc = kc_refs[g][:, r].reshape(cols, LANES).astype(BF16)
            s_r = lax.dot_general(q, kc, NT, preferred_element_type=F32) * scale
            ok = same_head & (t_row % dil == r) & (pos_col >= t_row // dil)
            s_past = jnp.where(ok, s_r, s_past)
        s_new = lax.dot_general(q, k_new, NT, preferred_element_type=F32) * scale
        s_new = jnp.where(new_head & (t_diff >= 0) & (t_diff % dil == 0), s_new, NEG_INF)
        m = jnp.maximum(jnp.max(s_past, axis=1, keepdims=True), jnp.max(s_new, axis=1, keepdims=True))
        p_past = jnp.exp(s_past - m)
        p_new = jnp.exp(s_new - m)
        l = jnp.sum(p_past, axis=1, keepdims=True) + jnp.sum(p_new, axis=1, keepdims=True)
        acc = jnp.dot(p_new.astype(BF16), v_new, preferred_element_type=F32)
        for r in range(n_res):
            vc = vc_refs[g][:, r].reshape(cols, LANES).astype(BF16)
            p_r = jnp.where(t_row % dil == r, p_past, 0.0).astype(BF16)
            acc = acc + jnp.dot(p_r, vc, preferred_element_type=F32)
        outs.append(acc / l)
        lses.append(m + jnp.log(l))
    m = functools.reduce(jnp.maximum, lses)
    w = [jnp.exp(x - m) for x in lses]
    den = functools.reduce(lambda a, b: a + b, w)
    o = functools.reduce(lambda a, b: a + b, [w[g] * outs[g] for g in range(n_g)]) / den
    o_ref[...] = jnp.concatenate([o[h * dec:(h + 1) * dec, :] for h in range(DIL_H)], axis=1).astype(o_ref.dtype)


def dil_sample_attention(q, k, v, caches_k, caches_v, layer, batch, dec):
    n_g = len(DIL)
    new_spec = pl.BlockSpec((n_g, dec, D), lambda b: (0, b, 0))
    cache_specs, views_k, views_v = [], [], []
    for g, dil in enumerate(DIL):
        window = caches_k[g].shape[2]
        shape = (caches_k[g].shape[0], batch, window // dil, dil, DIL_H, LANES)
        views_k.append(caches_k[g].reshape(shape))
        views_v.append(caches_v[g].reshape(shape))
        cache_specs.append(pl.BlockSpec((None, None, DIL_N, min(dil, dec), DIL_H, LANES),
                                        lambda b: (layer, b, 0, 0, 0, 0)))
    return pl.pallas_call(
        functools.partial(_dil_sample_kernel, dec),
        grid=(batch,),
        in_specs=[new_spec] * 3 + cache_specs * 2,
        out_specs=pl.BlockSpec((dec, D), lambda b: (b, 0)),
        out_shape=jax.ShapeDtypeStruct((batch * dec, D), F32),
        compiler_params=_cparams(1), name="dil_sample_attn",
    )(q, k, v, *views_k, *views_v)


def kernel(x_prompt, x_sample, cache_k_moba, cache_v_moba, page_table, state_hgrn, cache_k_dil0, cache_v_dil0,
           cache_k_dil1, cache_v_dil1, cache_k_dil2, cache_v_dil2, c_prompt, c_sample, norm_mix_g, norm_ffn_g,
           norm_final_g, ada_w, ada_b, moba_w_qkv, moba_w_o, hgrn_w_in, hgrn_w_o, hgrn_norm_g, hgrn_lower_bounds,
           dil_w_in, dil_w_o, ffn_w1, ffn_w2):
    batch, seq, _ = x_prompt.shape
    dbatch, dec, _ = x_sample.shape
    depth = ada_w.shape[0]
    past = page_table.shape[1] * PAGE
    pos_p = jnp.arange(seq, dtype=jnp.int32)
    pos_s = past + jnp.arange(dec, dtype=jnp.int32)

    stp = Stream(batch * seq, 512, seq, per_row=False)
    sts = Stream(dbatch * dec, 512, dec, per_row=True)
    stp_hg = Stream(batch * seq, 256, seq, per_row=False)
    sts_hg = Stream(dbatch * dec, 256, dec, per_row=True)

    mod_all = ada_modulation(jnp.concatenate([c_prompt, c_sample], axis=0), ada_w, ada_b)
    lb_all = jnp.cumsum(jax.nn.softmax(hgrn_lower_bounds.astype(F32), axis=0), axis=0)
    lb_all = lb_all - lb_all[0:1]

    n_pool = cache_k_moba.shape[1]
    k_cache_t = jnp.transpose(cache_k_moba, (0, 1, 3, 4, 2)).reshape(-1, n_pool, D, PAGE)
    v_cache_t = jnp.transpose(cache_v_moba, (0, 1, 3, 4, 2)).reshape(-1, n_pool, D, PAGE)
    dil_ck = (cache_k_dil0, cache_k_dil1, cache_k_dil2)
    dil_cv = (cache_v_dil0, cache_v_dil1, cache_v_dil2)

    tile_rows = lambda tab, st: jnp.tile(tab, (st.tm // tab.shape[0], 1))
    rope64_s = [tile_rows(t, sts) for t in _rope_row_tables(pos_s, MOBA_HD)]
    rope128_p = _rope_row_tables(pos_p, LANES)
    rope128_s = [tile_rows(t, sts) for t in _rope_row_tables(pos_s, LANES)]

    xp = x_prompt.reshape(batch * seq, D)
    xs = x_sample.reshape(dbatch * dec, D)
    moba_kp, moba_vp, moba_ks, moba_vs, hgrn_sp, hgrn_ss = [], [], [], [], [], []
    dil_kp, dil_vp, dil_ks, dil_vs = [], [], [], []

    for i in range(depth):
        kind, j = i % 3, i // 3
        mod_p = mod_all[i, :batch].reshape(batch, 1, -1)
        mod_s = jnp.repeat(mod_all[i, batch:], dec, axis=0)
        ng = norm_mix_g[i].reshape(1, D)
        hg_p = hg_s = None
        if kind == 0:
            w = moba_w_qkv[j].astype(BF16)
            qp, ktp, vtp = moba_prompt_proj(stp, xp, ng, mod_p, w[:, :D], w[:, D:].T, pos_p, batch, seq)
            op = moba_prompt_attention(qp.reshape(batch, seq, D), ktp, vtp).reshape(batch * seq, D)
            qs, ks, vs = rope_proj(sts, xs, ng, mod_s, w[None], rope64_s[0], rope64_s[1], MOBA_HD, F32)
            os_ = moba_sample_attention(qs[0], ks[0], vs[0], k_cache_t, v_cache_t, j, page_table, dec)
            heads_t = lambda a: a.reshape(batch, MOBA_H, MOBA_HD, seq).transpose(0, 3, 1, 2)
            moba_kp.append(heads_t(ktp))
            moba_vp.append(heads_t(vtp))
            moba_ks.append(ks[0].reshape(dbatch, dec, MOBA_H, MOBA_HD))
            moba_vs.append(vs[0].reshape(dbatch, dec, MOBA_H, MOBA_HD))
            wo = moba_w_o[j].astype(BF16)
        elif kind == 1:
            w = hgrn_w_in[j].astype(BF16)
            q_, lf_, k_, v_, gs_p = hgrn_proj(stp_hg, xp, ng, mod_p, w, lb_all[i])
            op, sp = hgrn_scan(q_, lf_, k_, v_, None, batch, seq)
            q_, lf_, k_, v_, gs_s = hgrn_proj(sts_hg, xs, ng, mod_s, w, lb_all[i])
            os_, ss = hgrn_scan(q_, lf_, k_, v_, state_hgrn[j], dbatch, dec)
            hgrn_sp.append(sp)
            hgrn_ss.append(ss)
            hn = hgrn_norm_g[j].reshape(1, D)
            hg_p, hg_s = (gs_p, hn), (gs_s, hn)
            wo = hgrn_w_o[j].astype(BF16)
        else:
            n_g = len(DIL)
            w = dil_w_in[j].astype(BF16).reshape(D, n_g, 3 * D).transpose(1, 0, 2)
            qp, kp, vp = rope_proj(stp, xp, ng, mod_p, w, rope128_p[0], rope128_p[1], LANES, F32)
            op = dil_prompt_attention(qp, kp, vp, batch, seq)
            qs, ks, vs = rope_proj(sts, xs, ng, mod_s, w, rope128_s[0], rope128_s[1], LANES, F32)
            os_ = dil_sample_attention(qs, ks, vs, dil_ck, dil_cv, j, dbatch, dec)
            kp4, vp4 = kp.reshape(n_g, batch, seq, D), vp.reshape(n_g, batch, seq, D)
            for g in range(n_g):
                keep = min(DIL_N * DIL[g], seq)
                heads = lambda a, b: a.reshape(b, -1, DIL_H, LANES)
                tail = lambda a: lax.slice(a, (g, 0, seq - keep, 0), (g + 1, batch, seq, D))
                dil_kp.append(heads(tail(kp4), batch))
                dil_vp.append(heads(tail(vp4), batch))
                window = dil_ck[g].shape[2]
                keep_s = min(DIL_N * DIL[g], window + dec)
                cat_k = jnp.concatenate([dil_ck[g][j], heads(ks[g], dbatch)], axis=1)
                cat_v = jnp.concatenate([dil_cv[g][j], heads(vs[g], dbatch)], axis=1)
                dil_ks.append(cat_k[:, window + dec - keep_s:])
                dil_vs.append(cat_v[:, window + dec - keep_s:])
            wo = dil_w_o[j].astype(BF16)
        nf = norm_ffn_g[i].reshape(1, D)
        w1, w2 = ffn_w1[i].astype(BF16), ffn_w2[i].astype(BF16)
        fin = norm_final_g.reshape(1, D) if i == depth - 1 else None
        xp = post_mixer(stp, op, xp, mod_p, wo, nf, w1, w2, hgrn_extra=hg_p, final_g=fin)
        xs = post_mixer(sts, os_, xs, mod_s, wo, nf, w1, w2, hgrn_extra=hg_s, final_g=fin)

    n_g = len(DIL)
    per_group = lambda lst: [jnp.stack(lst[g::n_g]) for g in range(n_g)]
    dkp, dvp, dks, dvs = per_group(dil_kp), per_group(dil_vp), per_group(dil_ks), per_group(dil_vs)
    return (xp.reshape(batch, seq, D), xs.reshape(dbatch, dec, D),
            jnp.stack(moba_kp), jnp.stack(moba_vp), jnp.stack(moba_ks), jnp.stack(moba_vs),
            jnp.stack(hgrn_sp), jnp.stack(hgrn_ss),
            dkp[0], dvp[0], dkp[1], dvp[1], dkp[2], dvp[2],
            dks[0], dvs[0], dks[1], dvs[1], dks[2], dvs[2])
```
